```python
import math
import jax, jax.numpy as jnp
from jax import lax
import numpy as np

D_MODEL = 1024
BATCH = 2
SEQ = 8192
DEPTH = 2
DEC_BATCH = 32
DEC_SEQ = 8
PAST_LEN = 16384
PAGE_SIZE = 128

N_MIXERS = 2
N_SSM_LAYERS = (DEPTH + 1) // 2
N_ATTN_LAYERS = DEPTH // 2
SSM_EXPAND = 2
SSM_D_INNER = SSM_EXPAND * D_MODEL
SSM_HEAD_DIM = 64
SSM_HEADS = SSM_D_INNER // SSM_HEAD_DIM
SSM_GROUPS = 8
SSM_HPG = SSM_HEADS // SSM_GROUPS
SSM_STATE = 128
SSM_CONV = 4
SSM_CONV_DIM = SSM_D_INNER + 2 * SSM_GROUPS * SSM_STATE
SSM_IN_DIM = SSM_D_INNER + SSM_CONV_DIM + SSM_HEADS
SSM_CHUNK = 128
ATTN_HEADS = 16
ATTN_HEAD_DIM = 64
ATTN_WIDTH = ATTN_HEADS * ATTN_HEAD_DIM
ATTN_IN_DIM = 4 * ATTN_WIDTH + ATTN_HEADS
Q_BLOCK = 128
FORGET_BIAS_INIT = 3.0
RMS_EPS = 1e-6

kernel_name = 'fox_ssd_hybrid_step'


def rms_norm(x, g):
    xf = x.astype(jnp.float32)
    y = xf * lax.rsqrt(jnp.mean(xf * xf, axis=-1, keepdims=True) + RMS_EPS)
    return (y * g.astype(jnp.float32)).astype(x.dtype)


def gated_group_rms_norm(y, z, g):
    h = y.astype(jnp.float32) * jax.nn.silu(z.astype(jnp.float32))
    hg = h.reshape(h.shape[:-1] + (SSM_GROUPS, SSM_D_INNER // SSM_GROUPS))
    hg = hg * lax.rsqrt(jnp.mean(hg * hg, axis=-1, keepdims=True) + RMS_EPS)
    return (hg.reshape(h.shape) * g.astype(jnp.float32)).astype(y.dtype)


def causal_conv(xpad, w, bias, length):
    acc = bias.astype(jnp.float32)
    for k in range(SSM_CONV):
        acc = acc + xpad[:, k:k + length].astype(jnp.float32) * w[k].astype(jnp.float32)
    return acc


def ssd_scan(x, dt, a, B, C, h0):
    b, L = x.shape[:2]
    Q = SSM_CHUNK if L % SSM_CHUNK == 0 else L
    nc = L // Q
    x = x.reshape((b, nc, Q) + x.shape[2:])
    dt = dt.reshape((b, nc, Q) + dt.shape[2:])
    B = B.reshape((b, nc, Q) + B.shape[2:])
    C = C.reshape((b, nc, Q) + C.shape[2:])
    acum = jnp.cumsum(dt * a, axis=2)
    xdt = x * dt[..., None]
    seg = acum[:, :, :, None] - acum[:, :, None, :]
    mask = jnp.tril(jnp.ones((Q, Q), dtype=bool))[:, :, None, None]
    lmat = jnp.exp(jnp.where(mask, seg, -jnp.inf))
    cb = jnp.einsum('bctgn,bcsgn->bctsg', C, B)
    y_diag = jnp.einsum('bctsg,bctsgr,bcsgrp->bctgrp', cb, lmat, xdt)
    decay = jnp.exp(acum[:, :, -1:] - acum)
    states = jnp.einsum('bcsgn,bcsgr,bcsgrp->bcgrpn', B, decay, xdt)
    chunk_decay = jnp.exp(acum[:, :, -1])

    def step(h, inp):
        st, dec = inp
        return h * dec[..., None, None] + st, h

    h_last, h_prev = lax.scan(step, h0, (jnp.moveaxis(states, 1, 0), jnp.moveaxis(chunk_decay, 1, 0)))
    y_off = jnp.einsum('bctgn,cbgrpn,bctgr->bctgrp', C, h_prev, jnp.exp(acum))
    y = (y_diag + y_off).reshape((b, L) + x.shape[3:])
    return y, h_last


def ssd_mixer(u, conv_state, h0, w_in, conv_w, conv_b, dt_bias, a_log, d_skip, norm_w, w_out):
    b, L, _ = u.shape
    zxbcdt = u @ w_in
    z = zxbcdt[..., :SSM_D_INNER]
    xbc = zxbcdt[..., SSM_D_INNER:SSM_D_INNER + SSM_CONV_DIM]
    dt_raw = zxbcdt[..., SSM_D_INNER + SSM_CONV_DIM:]
    xpad = jnp.concatenate([conv_state.astype(xbc.dtype), xbc], axis=1)
    new_conv = xpad[:, -(SSM_CONV - 1):]
    xbc = jax.nn.silu(causal_conv(xpad, conv_w, conv_b, L))
    gn = SSM_GROUPS * SSM_STATE
    xs = xbc[..., :SSM_D_INNER].reshape(b, L, SSM_GROUPS, SSM_HPG, SSM_HEAD_DIM)
    Bm = xbc[..., SSM_D_INNER:SSM_D_INNER + gn].reshape(b, L, SSM_GROUPS, SSM_STATE)
    Cm = xbc[..., SSM_D_INNER + gn:].reshape(b, L, SSM_GROUPS, SSM_STATE)
    dt = jax.nn.softplus(dt_raw.astype(jnp.float32) + dt_bias.astype(jnp.float32)).reshape(b, L, SSM_GROUPS, SSM_HPG)
    a = -jnp.exp(a_log.astype(jnp.float32)).reshape(SSM_GROUPS, SSM_HPG)
    h0g = h0.astype(jnp.float32).reshape(b, SSM_GROUPS, SSM_HPG, SSM_HEAD_DIM, SSM_STATE)
    y, h = ssd_scan(xs, dt, a, Bm, Cm, h0g)
    y = y + d_skip.astype(jnp.float32).reshape(SSM_GROUPS, SSM_HPG)[..., None] * xs
    y = y.reshape(b, L, SSM_D_INNER).astype(u.dtype)
    y = gated_group_rms_norm(y, z, norm_w)
    h = h.reshape(b, SSM_HEADS, SSM_HEAD_DIM, SSM_STATE).astype(u.dtype)
    return y @ w_out, new_conv, h


def fox_project(u, w_in, b_f):
    b, L, _ = u.shape
    proj = u @ w_in
    q, k, v, z = [proj[..., i * ATTN_WIDTH:(i + 1) * ATTN_WIDTH].reshape(b, L, ATTN_HEADS, ATTN_HEAD_DIM) for i in range(4)]
    logf = jax.nn.log_sigmoid(proj[..., 4 * ATTN_WIDTH:].astype(jnp.float32) + b_f.astype(jnp.float32))
    return q, k, v, z, logf


def fox_prompt_attention(q, k, v, logf):
    b, L, H, Dh = q.shape
    scale = Dh ** -0.5
    bq = min(Q_BLOCK, L)
    nb = L // bq
    c = jnp.cumsum(logf, axis=1).transpose(0, 2, 1)
    key_pos = jnp.arange(L)

    def block(i):
        start = i * bq
        qb = lax.dynamic_slice_in_dim(q, start, bq, axis=1)
        cq = lax.dynamic_slice_in_dim(c, start, bq, axis=2)
        s = jnp.einsum('bqhd,bkhd->bhqk', qb, k, preferred_element_type=jnp.float32) * scale
        s = s + cq[..., :, None] - c[..., None, :]
        mask = (start + jnp.arange(bq))[:, None] >= key_pos[None, :]
        p = jax.nn.softmax(jnp.where(mask, s, -jnp.inf), axis=-1)
        return jnp.einsum('bhqk,bkhd->bqhd', p.astype(v.dtype), v)

    o = lax.map(block, jnp.arange(nb))
    return jnp.moveaxis(o, 0, 1).reshape(b, L, H, Dh)


def fox_sample_attention(q, k, v, logf, k_past, v_past, logf_past):
    Dh = q.shape[-1]
    scale = Dh ** -0.5
    Lq = q.shape[1]
    P = k_past.shape[1]
    lfp = logf_past.astype(jnp.float32)
    shifted = jnp.concatenate([lfp[:, 1:], jnp.zeros_like(lfp[:, :1])], axis=1)
    r = lax.cumsum(shifted, axis=1, reverse=True).transpose(0, 2, 1)
    cn = jnp.cumsum(logf, axis=1).transpose(0, 2, 1)
    s_past = jnp.einsum('bqhd,bkhd->bhqk', q, k_past, preferred_element_type=jnp.float32) * scale
    s_past = s_past + cn[..., :, None] + r[..., None, :]
    s_new = jnp.einsum('bqhd,bkhd->bhqk', q, k, preferred_element_type=jnp.float32) * scale
    s_new = s_new + cn[..., :, None] - cn[..., None, :]
    mask = jnp.tril(jnp.ones((Lq, Lq), dtype=bool))
    s_new = jnp.where(mask, s_new, -jnp.inf)
    p = jax.nn.softmax(jnp.concatenate([s_past, s_new], axis=-1), axis=-1)
    o = jnp.einsum('bhqk,bkhd->bqhd', p[..., :P].astype(v.dtype), v_past)
    o = o + jnp.einsum('bhqk,bkhd->bqhd', p[..., P:].astype(v.dtype), v)
    return o


def fox_output(o, z, w_out):
    b, L = o.shape[:2]
    g = (o.astype(jnp.float32) * jax.nn.silu(z.astype(jnp.float32))).astype(z.dtype)
    return g.reshape(b, L, ATTN_WIDTH) @ w_out


def setup_inputs(seed: int = 0) -> dict:
    key = jax.random.key(seed)
    ks = jax.random.split(key, 24)
    f32 = jnp.float32
    n_pages = PAST_LEN // PAGE_SIZE
    n_used = DEC_BATCH * n_pages
    n_pool = n_used + n_used // 4
    nrm = lambda k, s: jax.random.normal(k, s, f32)
    x_prompt = nrm(ks[0], (BATCH, SEQ, D_MODEL))
    x_sample = nrm(ks[1], (DEC_BATCH, DEC_SEQ, D_MODEL))
    state_ssm = 0.1 * nrm(ks[2], (N_SSM_LAYERS, DEC_BATCH, SSM_HEADS, SSM_HEAD_DIM, SSM_STATE))
    state_conv = nrm(ks[3], (N_SSM_LAYERS, DEC_BATCH, SSM_CONV - 1, SSM_CONV_DIM))
    cache_k = nrm(ks[4], (N_ATTN_LAYERS, n_pool, PAGE_SIZE, ATTN_HEADS, ATTN_HEAD_DIM))
    cache_v = nrm(ks[5], (N_ATTN_LAYERS, n_pool, PAGE_SIZE, ATTN_HEADS, ATTN_HEAD_DIM))
    cache_logf = jax.nn.log_sigmoid(FORGET_BIAS_INIT + nrm(ks[6], (N_ATTN_LAYERS, n_pool, PAGE_SIZE, ATTN_HEADS)))
    page_table = jax.random.permutation(ks[7], n_pool)[:n_used].reshape(DEC_BATCH, n_pages).astype(jnp.int32)
    norm_pre = 1.0 + 0.02 * nrm(ks[8], (DEPTH, D_MODEL))
    norm_post = 1.0 + 0.02 * nrm(ks[9], (DEPTH, D_MODEL))
    ssm_w_in = nrm(ks[10], (N_SSM_LAYERS, D_MODEL, SSM_IN_DIM)) * D_MODEL ** -0.5
    ssm_conv_w = nrm(ks[11], (N_SSM_LAYERS, SSM_CONV, SSM_CONV_DIM)) * SSM_CONV ** -0.5
    ssm_conv_b = 0.02 * nrm(ks[12], (N_SSM_LAYERS, SSM_CONV_DIM))
    dt0 = jnp.exp(jax.random.uniform(ks[13], (N_SSM_LAYERS, SSM_HEADS), f32, math.log(1e-3), math.log(1e-1)))
    ssm_dt_bias = dt0 + jnp.log(-jnp.expm1(-dt0))
    ssm_a_log = jnp.log(jax.random.uniform(ks[14], (N_SSM_LAYERS, SSM_HEADS), f32, 1.0, 16.0))
    ssm_d = 1.0 + 0.02 * nrm(ks[15], (N_SSM_LAYERS, SSM_HEADS))
    ssm_norm_w = 1.0 + 0.02 * nrm(ks[16], (N_SSM_LAYERS, SSM_D_INNER))
    ssm_w_out = nrm(ks[17], (N_SSM_LAYERS, SSM_D_INNER, D_MODEL)) * SSM_D_INNER ** -0.5
    attn_w_in = nrm(ks[18], (N_ATTN_LAYERS, D_MODEL, ATTN_IN_DIM)) * D_MODEL ** -0.5
    attn_b_f = FORGET_BIAS_INIT + 0.1 * nrm(ks[19], (N_ATTN_LAYERS, ATTN_HEADS))
    attn_w_out = nrm(ks[20], (N_ATTN_LAYERS, ATTN_WIDTH, D_MODEL)) * ATTN_WIDTH ** -0.5
    return {'x_prompt': x_prompt, 'x_sample': x_sample, 'state_ssm': state_ssm, 'state_conv': state_conv,
            'cache_k': cache_k, 'cache_v': cache_v, 'cache_logf': cache_logf, 'page_table': page_table,
            'norm_pre': norm_pre, 'norm_post': norm_post,
            'ssm_w_in': ssm_w_in, 'ssm_conv_w': ssm_conv_w, 'ssm_conv_b': ssm_conv_b, 'ssm_dt_bias': ssm_dt_bias,
            'ssm_a_log': ssm_a_log, 'ssm_d': ssm_d, 'ssm_norm_w': ssm_norm_w, 'ssm_w_out': ssm_w_out,
            'attn_w_in': attn_w_in, 'attn_b_f': attn_b_f, 'attn_w_out': attn_w_out}


def reference(x_prompt, x_sample, state_ssm, state_conv, cache_k, cache_v, cache_logf, page_table,
              norm_pre, norm_post, ssm_w_in, ssm_conv_w, ssm_conv_b, ssm_dt_bias, ssm_a_log, ssm_d,
              ssm_norm_w, ssm_w_out, attn_w_in, attn_b_f, attn_w_out):
    hp, hs = x_prompt, x_sample
    bp = x_prompt.shape[0]
    bd = x_sample.shape[0]
    kp_l, vp_l, fp_l, ks_l, vs_l, fs_l = [], [], [], [], [], []
    sp_l, cp_l, ss_l, cs_l = [], [], [], []
    for i in range(DEPTH):
        up = rms_norm(hp, norm_pre[i])
        us = rms_norm(hs, norm_pre[i])
        j = i // N_MIXERS
        if i % N_MIXERS == 0:
            prm = (ssm_w_in[j], ssm_conv_w[j], ssm_conv_b[j], ssm_dt_bias[j], ssm_a_log[j], ssm_d[j], ssm_norm_w[j], ssm_w_out[j])
            conv0 = jnp.zeros((bp, SSM_CONV - 1, SSM_CONV_DIM), up.dtype)
            h0 = jnp.zeros((bp, SSM_HEADS, SSM_HEAD_DIM, SSM_STATE), jnp.float32)
            op, cpn, spn = ssd_mixer(up, conv0, h0, *prm)
            osm, csn, ssn = ssd_mixer(us, state_conv[j], state_ssm[j], *prm)
            sp_l.append(spn)
            cp_l.append(cpn)
            ss_l.append(ssn)
            cs_l.append(csn)
        else:
            q, k, v, z, lf = fox_project(up, attn_w_in[j], attn_b_f[j])
            op = fox_output(fox_prompt_attention(q, k, v, lf), z, attn_w_out[j])
            qs, kn, vn, zs, lfs = fox_project(us, attn_w_in[j], attn_b_f[j])
            k_past = cache_k[j, page_table].reshape(bd, -1, ATTN_HEADS, ATTN_HEAD_DIM)
            v_past = cache_v[j, page_table].reshape(bd, -1, ATTN_HEADS, ATTN_HEAD_DIM)
            f_past = cache_logf[j, page_table].reshape(bd, -1, ATTN_HEADS)
            osm = fox_output(fox_sample_attention(qs, kn, vn, lfs, k_past, v_past, f_past), zs, attn_w_out[j])
            kp_l.append(k)
            vp_l.append(v)
            fp_l.append(lf.astype(cache_logf.dtype))
            ks_l.append(kn)
            vs_l.append(vn)
            fs_l.append(lfs.astype(cache_logf.dtype))
        hp = hp + rms_norm(op, norm_post[i])
        hs = hs + rms_norm(osm, norm_post[i])
    return (hp, hs, jnp.stack(kp_l), jnp.stack(vp_l), jnp.stack(fp_l), jnp.stack(ks_l), jnp.stack(vs_l), jnp.stack(fs_l),
            jnp.stack(sp_l), jnp.stack(cp_l), jnp.stack(ss_l), jnp.stack(cs_l))
```

```python
import functools

import jax
import jax.numpy as jnp
from jax import lax
from jax.experimental import pallas as pl
from jax.experimental.pallas import tpu as pltpu

F32 = jnp.float32
BF16 = jnp.bfloat16

D_MODEL = 1024
SSM_D_INNER = 2048
SSM_HEAD_DIM = 64
SSM_HEADS = 32
SSM_GROUPS = 8
SSM_HPG = SSM_HEADS // SSM_GROUPS
SSM_STATE = 128
SSM_CONV = 4
SSM_CONV_DIM = SSM_D_INNER + 2 * SSM_GROUPS * SSM_STATE
SSM_CHUNK = 128
ATTN_HEADS = 16
ATTN_HEAD_DIM = 64
ATTN_WIDTH = ATTN_HEADS * ATTN_HEAD_DIM
PAGE_SIZE = 128
RMS_EPS = 1e-6
LANES = 128
SUBLANES = 8
NEG_BIG = -1e30
VMEM_LIMIT = 56 * 1024 * 1024

_NT = (((1,), (1,)), ((), ()))
_TN = (((0,), (0,)), ((), ()))


def _silu(x):
    return x / (1.0 + jnp.exp(-x))


def _softplus(x):
    return jnp.maximum(x, 0.0) + jnp.log1p(jnp.exp(-jnp.abs(x)))


def _log_sigmoid(x):
    return jnp.minimum(x, 0.0) - jnp.log1p(jnp.exp(-jnp.abs(x)))


def _rms(x):
    return x * lax.rsqrt(jnp.mean(x * x, axis=-1, keepdims=True) + RMS_EPS)


def _split3(x):
    hi = x.astype(BF16)
    r1 = x - hi.astype(F32)
    mid = r1.astype(BF16)
    lo = (r1 - mid.astype(F32)).astype(BF16)
    return hi, mid, lo


def _dot3(x, tri):
    n = x.shape[0]
    hi, mid, lo = _split3(x)
    r = jnp.dot(jnp.concatenate([hi, mid, lo], axis=0), tri, preferred_element_type=F32)
    return r[:n] + r[n:2 * n] + r[2 * n:]


def _tri(n, fn):
    r = lax.broadcasted_iota(jnp.int32, (n, n), 0)
    c = lax.broadcasted_iota(jnp.int32, (n, n), 1)
    return jnp.where(fn(r, c), 1.0, 0.0).astype(BF16)


def _cparams(sem, **kw):
    return pltpu.CompilerParams(dimension_semantics=sem, vmem_limit_bytes=VMEM_LIMIT, **kw)


def _inproj0_kernel(x_ref, g_ref, wm_ref, wd_ref, om_ref, od_ref, un_ref):
    @pl.when(pl.program_id(1) == 0)
    def _():
        un = (_rms(x_ref[...]) * g_ref[...]).astype(BF16)
        un_ref[...] = un
        od_ref[...] = jnp.dot(un, wd_ref[...], preferred_element_type=F32)

    om_ref[...] = jnp.dot(un_ref[...], wm_ref[...], preferred_element_type=F32)


def _inproj0(x, g, wm, wd, tm, tn=1024):
    m, d = x.shape
    n = wm.shape[1]
    return pl.pallas_call(
        _inproj0_kernel,
        grid=(m // tm, n // tn),
        in_specs=[
            pl.BlockSpec((tm, d), lambda i, j: (i, 0)),
            pl.BlockSpec((1, d), lambda i, j: (0, 0)),
            pl.BlockSpec((d, tn), lambda i, j: (0, j)),
            pl.BlockSpec((d, LANES), lambda i, j: (0, 0)),
        ],
        out_specs=[
            pl.BlockSpec((tm, tn), lambda i, j: (i, j)),
            pl.BlockSpec((tm, LANES), lambda i, j: (i, 0)),
        ],
        out_shape=[jax.ShapeDtypeStruct((m, n), F32), jax.ShapeDtypeStruct((m, LANES), F32)],
        scratch_shapes=[pltpu.VMEM((tm, d), BF16)],
        compiler_params=_cparams(("parallel", "arbitrary")),
        name="inproj0",
    )(x, g, wm, wd)


def _ssd_kernel(xbc_ref, dt_ref, cw_ref, cb_ref, dtb_ref, alog_ref, dsk_ref, conv0_ref, h0_ref,
                y_ref, hout_ref, convout_ref,
                xpad_ref, act_ref, h_ref, act_t_ref, *, rows_in, n_valid):
    q = SSM_CHUNK
    c = pl.program_id(1)
    last_c = pl.num_programs(1) - 1
    halo = SUBLANES
    tail0 = halo - (SSM_CONV - 1)

    @pl.when(c == 0)
    def _():
        xpad_ref[0:halo, :] = jnp.zeros((halo, SSM_CONV_DIM), F32)
        xpad_ref[tail0:halo, :] = conv0_ref[0]
        if rows_in < q:
            xpad_ref[halo + rows_in:halo + q, :] = jnp.zeros((q - rows_in, SSM_CONV_DIM), F32)
        h_ref[...] = h0_ref[0]

    xpad_ref[halo:halo + rows_in, :] = xbc_ref[...]

    cchunk = 512
    for j0 in range(0, SSM_CONV_DIM, cchunk):
        acc = jnp.broadcast_to(cb_ref[:, j0:j0 + cchunk], (q, cchunk))
        for k in range(SSM_CONV):
            acc = acc + xpad_ref[tail0 + k:tail0 + k + q, j0:j0 + cchunk] * cw_ref[k:k + 1, j0:j0 + cchunk]
        act_ref[:, j0:j0 + cchunk] = _silu(acc)

    new_tail = xpad_ref[halo + n_valid - (SSM_CONV - 1):halo + n_valid, :]
    xpad_ref[tail0:halo, :] = new_tail

    @pl.when(c == last_c)
    def _():
        convout_ref[0] = new_tail

    row = lax.broadcasted_iota(jnp.int32, (q, LANES), 0)
    dt_raw = dt_ref[...]
    if rows_in < q:
        dt_raw = jnp.concatenate([dt_raw, jnp.zeros((q - rows_in, LANES), F32)], axis=0)
    dt = _softplus(dt_raw + dtb_ref[...])
    if n_valid < q:
        dt = jnp.where(row < n_valid, dt, 0.0)
    adt = dt * (-jnp.exp(alog_ref[...]))
    acum = adt
    k = 1
    while k < q:
        acum = acum + jnp.where(row >= k, pltpu.roll(acum, k, axis=0), 0.0)
        k *= 2
    act_t_ref[...] = acum.T
    eac = jnp.exp(acum)
    last = acum[q - 1:q, :]
    dec = jnp.exp(last - acum)
    cdec = jnp.exp(last)
    dsk = dsk_ref[...]

    r_i = lax.broadcasted_iota(jnp.int32, (q, q), 0)
    c_i = lax.broadcasted_iota(jnp.int32, (q, q), 1)
    causal = r_i >= c_i

    gn = SSM_GROUPS * SSM_STATE
    for g in range(SSM_GROUPS):
        b_g = act_ref[:, SSM_D_INNER + g * SSM_STATE:SSM_D_INNER + (g + 1) * SSM_STATE].astype(BF16)
        c_g = act_ref[:, SSM_D_INNER + gn + g * SSM_STATE:SSM_D_INNER + gn + (g + 1) * SSM_STATE].astype(BF16)
        cb = lax.dot_general(c_g, b_g, _NT, preferred_element_type=F32)
        for r in range(SSM_HPG):
            h = g * SSM_HPG + r
            seg = acum[:, h:h + 1] - act_t_ref[h:h + 1, :]
            lm = jnp.exp(jnp.where(causal, seg, -jnp.inf))
            mat = (cb * lm).astype(BF16)
            xs_h = act_ref[:, h * SSM_HEAD_DIM:(h + 1) * SSM_HEAD_DIM]
            xdt = xs_h * dt[:, h:h + 1]
            y_diag = jnp.dot(mat, xdt.astype(BF16), preferred_element_type=F32)
            hs = h_ref[h]
            y_off = lax.dot_general(c_g, hs.astype(BF16), _NT, preferred_element_type=F32)
            y = y_diag + y_off * eac[:, h:h + 1] + xs_h * dsk[:, h:h + 1]
            y_ref[:, h * SSM_HEAD_DIM:(h + 1) * SSM_HEAD_DIM] = y[:rows_in]
            xdtd = (xdt * dec[:, h:h + 1]).astype(BF16)
            h_new = lax.dot_general(xdtd, b_g, _TN, preferred_element_type=F32)
            h_ref[h] = hs * cdec[:, h:h + 1] + h_new

    @pl.when(c == last_c)
    def _():
        hout_ref[0] = h_ref[...]


def _ssd(om, od, cw, cb, dtb, alog, dsk, conv0, h0, nb, nc, rows_in, n_valid):
    m = om.shape[0]
    kern = functools.partial(_ssd_kernel, rows_in=rows_in, n_valid=n_valid)
    vec = lambda: pl.BlockSpec((1, LANES), lambda b, c: (0, 0))
    return pl.pallas_call(
        kern,
        grid=(nb, nc),
        in_specs=[
            pl.BlockSpec((rows_in, SSM_CONV_DIM), lambda b, c: (b * nc + c, 0)),
            pl.BlockSpec((rows_in, LANES), lambda b, c: (b * nc + c, 0)),
            pl.BlockSpec((SSM_CONV, SSM_CONV_DIM), lambda b, c: (0, 0)),
            pl.BlockSpec((1, SSM_CONV_DIM), lambda b, c: (0, 0)),
            vec(), vec(), vec(),
            pl.BlockSpec((1, SSM_CONV - 1, SSM_CONV_DIM), lambda b, c: (b, 0, 0)),
            pl.BlockSpec((1, SSM_HEADS, SSM_HEAD_DIM, SSM_STATE), lambda b, c: (b, 0, 0, 0)),
        ],
        out_specs=[
            pl.BlockSpec((rows_in, SSM_D_INNER), lambda b, c: (b * nc + c, 0)),
            pl.BlockSpec((1, SSM_HEADS, SSM_HEAD_DIM, SSM_STATE), lambda b, c: (b, 0, 0, 0)),
            pl.BlockSpec((1, SSM_CONV - 1, SSM_CONV_DIM), lambda b, c: (b, 0, 0)),
        ],
        out_shape=[
            jax.ShapeDtypeStruct((m, SSM_D_INNER), F32),
            jax.ShapeDtypeStruct((nb, SSM_HEADS, SSM_HEAD_DIM, SSM_STATE), F32),
            jax.ShapeDtypeStruct((nb, SSM_CONV - 1, SSM_CONV_DIM), F32),
        ],
        scratch_shapes=[
            pltpu.VMEM((SUBLANES + SSM_CHUNK, SSM_CONV_DIM), F32),
            pltpu.VMEM((SSM_CHUNK, SSM_CONV_DIM), F32),
            pltpu.VMEM((SSM_HEADS, SSM_HEAD_DIM, SSM_STATE), F32),
            pltpu.VMEM((LANES, SSM_CHUNK), F32),
        ],
        compiler_params=_cparams(("arbitrary", "arbitrary")),
        name="ssd",
    )(om, od, cw, cb, dtb, alog, dsk, conv0, h0)


def _outproj0_kernel(y_ref, z_ref, nw_ref, w_ref, np_ref, x_ref, o_ref):
    h = y_ref[...] * _silu(z_ref[...])
    gw = SSM_D_INNER // SSM_GROUPS
    hn = jnp.concatenate([_rms(h[:, g * gw:(g + 1) * gw]) for g in range(SSM_GROUPS)], axis=1)
    hn = (hn * nw_ref[...]).astype(BF16)
    o = jnp.dot(hn, w_ref[...], preferred_element_type=F32)
    o_ref[...] = x_ref[...] + _rms(o) * np_ref[...]


def _outproj0(y, om, nw, w, npost, x, tm):
    m = y.shape[0]
    zblk = SSM_CONV_DIM // SSM_D_INNER
    return pl.pallas_call(
        _outproj0_kernel,
        grid=(m // tm,),
        in_specs=[
            pl.BlockSpec((tm, SSM_D_INNER), lambda i: (i, 0)),
            pl.BlockSpec((tm, SSM_D_INNER), lambda i: (i, zblk)),
            pl.BlockSpec((1, SSM_D_INNER), lambda i: (0, 0)),
            pl.BlockSpec((SSM_D_INNER, D_MODEL), lambda i: (0, 0)),
            pl.BlockSpec((1, D_MODEL), lambda i: (0, 0)),
            pl.BlockSpec((tm, D_MODEL), lambda i: (i, 0)),
        ],
        out_specs=pl.BlockSpec((tm, D_MODEL), lambda i: (i, 0)),
        out_shape=jax.ShapeDtypeStruct((m, D_MODEL), F32),
        compiler_params=_cparams(("parallel",)),
        name="outproj0",
    )(y, om, nw, w, npost, x)


def _inproj1p_kernel(x_ref, g_ref, wq_ref, wkt_ref, wvt_ref, wz_ref, wft_ref, bf_ref,
                     q_ref, kt_ref, ktb_ref, vt_ref, vtb_ref, z_ref, lft_ref):
    un = (_rms(x_ref[...]) * g_ref[...]).astype(BF16)
    scale = ATTN_HEAD_DIM ** -0.5
    q_ref[...] = (jnp.dot(un, wq_ref[...], preferred_element_type=F32) * scale).astype(BF16)
    kt = lax.dot_general(wkt_ref[...], un, _NT, preferred_element_type=F32)
    kt_ref[0] = kt
    ktb_ref[0] = kt.astype(BF16)
    vt = lax.dot_general(wvt_ref[...], un, _NT, preferred_element_type=F32)
    vt_ref[0] = vt
    vtb_ref[0] = vt.astype(BF16)
    z_ref[...] = jnp.dot(un, wz_ref[...], preferred_element_type=F32)
    ft = lax.dot_general(wft_ref[...], un, _NT, preferred_element_type=F32)
    lft_ref[0] = _log_sigmoid(ft + bf_ref[...])


def _inproj1p(x, g, wq, wkt, wvt, wz, wft, bfc, nb, seq, tm):
    m, d = x.shape
    nq = seq // tm
    w = ATTN_WIDTH
    full = lambda r, c: pl.BlockSpec((r, c), lambda b, i: (0, 0))
    tok = lambda c: pl.BlockSpec((tm, c), lambda b, i: (b * nq + i, 0))
    feat = lambda r: pl.BlockSpec((1, r, tm), lambda b, i: (b, 0, i))
    return pl.pallas_call(
        _inproj1p_kernel,
        grid=(nb, nq),
        in_specs=[tok(d), full(1, d), full(d, w), full(w, d), full(w, d), full(d, w),
                  full(ATTN_HEADS, d), full(ATTN_HEADS, 1)],
        out_specs=[tok(w), feat(w), feat(w), feat(w), feat(w), tok(w), feat(ATTN_HEADS)],
        out_shape=[
            jax.ShapeDtypeStruct((m, w), BF16),
            jax.ShapeDtypeStruct((nb, w, seq), F32),
            jax.ShapeDtypeStruct((nb, w, seq), BF16),
            jax.ShapeDtypeStruct((nb, w, seq), F32),
            jax.ShapeDtypeStruct((nb, w, seq), BF16),
            jax.ShapeDtypeStruct((m, w), F32),
            jax.ShapeDtypeStruct((nb, ATTN_HEADS, seq), F32),
        ],
        compiler_params=_cparams(("parallel", "parallel")),
        name="inproj1p",
    )(x, g, wq, wkt, wvt, wz, wft, bfc)


def _inproj1s_kernel(x_ref, g_ref, w_ref, wf_ref, bf_ref, o_ref, lf_ref):
    un = (_rms(x_ref[...]) * g_ref[...]).astype(BF16)
    o_ref[...] = jnp.dot(un, w_ref[...], preferred_element_type=F32)
    lf_ref[...] = _log_sigmoid(jnp.dot(un, wf_ref[...], preferred_element_type=F32) + bf_ref[...])


def _inproj1s(x, g, w, wf, bfr):
    m, d = x.shape
    n = w.shape[1]
    return pl.pallas_call(
        _inproj1s_kernel,
        grid=(1,),
        in_specs=[pl.BlockSpec((m, d), lambda i: (0, 0)), pl.BlockSpec((1, d), lambda i: (0, 0)),
                  pl.BlockSpec((d, n), lambda i: (0, 0)), pl.BlockSpec((d, LANES), lambda i: (0, 0)),
                  pl.BlockSpec((1, LANES), lambda i: (0, 0))],
        out_specs=[pl.BlockSpec((m, n), lambda i: (0, 0)), pl.BlockSpec((m, LANES), lambda i: (0, 0))],
        out_shape=[jax.ShapeDtypeStruct((m, n), F32), jax.ShapeDtypeStruct((m, LANES), F32)],
        compiler_params=_cparams(("arbitrary",)),
        name="inproj1s",
    )(x, g, w, wf, bfr)


def _cumsum_kernel(x_ref, o_ref):
    nblk = x_ref.shape[2] // LANES
    upper = _tri(LANES, lambda r, c: r <= c)

    def body(j, carry):
        off = pl.multiple_of(j * LANES, LANES)
        loc = _dot3(x_ref[0, :, pl.ds(off, LANES)], upper)
        o_ref[0, :, pl.ds(off, LANES)] = loc + carry
        return carry + loc[:, LANES - 1:LANES]

    lax.fori_loop(0, nblk, body, jnp.zeros((x_ref.shape[1], 1), F32))


def _cumsum(x):
    nb, h, seq = x.shape
    return pl.pallas_call(
        _cumsum_kernel,
        grid=(nb,),
        in_specs=[pl.BlockSpec((1, h, seq), lambda b: (b, 0, 0))],
        out_specs=pl.BlockSpec((1, h, seq), lambda b: (b, 0, 0)),
        out_shape=jax.ShapeDtypeStruct((nb, h, seq), F32),
        compiler_params=_cparams(("parallel",)),
        name="lf_cumsum",
    )(x)


def _flash_kernel(q_ref, kt_ref, vt_ref, cq_ref, ck_ref, o_ref, m_ref, l_ref, acc_ref, cqt_ref, *, tq, tk):
    qi = pl.program_id(1)
    kj = pl.program_id(2)
    hd = ATTN_HEAD_DIM

    @pl.when(kj == 0)
    def _():
        m_ref[...] = jnp.full(m_ref.shape, NEG_BIG, F32)
        l_ref[...] = jnp.zeros(l_ref.shape, F32)
        acc_ref[...] = jnp.zeros(acc_ref.shape, F32)
        cq = jnp.concatenate([cq_ref[0], jnp.zeros((LANES - ATTN_HEADS, tq), F32)], axis=0)
        cqt_ref[...] = cq.T

    @pl.when(kj <= qi)
    def _():
        rows = qi * tq + lax.broadcasted_iota(jnp.int32, (tq, tk), 0)
        cols = kj * tk + lax.broadcasted_iota(jnp.int32, (tq, tk), 1)
        keep = cols <= rows
        for h in range(ATTN_HEADS):
            sl = slice(h * hd, (h + 1) * hd)
            s = jnp.dot(q_ref[:, sl], kt_ref[0, sl, :], preferred_element_type=F32)
            s = s + cqt_ref[:, h:h + 1] - ck_ref[0, h:h + 1, :]
            s = jnp.where(keep, s, NEG_BIG)
            m_prev = m_ref[h]
            m_new = jnp.maximum(m_prev, jnp.max(s, axis=1, keepdims=True))
            alpha = jnp.exp(m_prev - m_new)
            p = jnp.exp(s - m_new)
            l_ref[h] = alpha * l_ref[h] + jnp.sum(p, axis=1, keepdims=True)
            pv = lax.dot_general(p.astype(BF16), vt_ref[0, sl, :], _NT, preferred_element_type=F32)
            acc_ref[:, sl] = alpha * acc_ref[:, sl] + pv
            m_ref[h] = m_new

    @pl.when(kj == qi)
    def _():
        for h in range(ATTN_HEADS):
            sl = slice(h * hd, (h + 1) * hd)
            o_ref[:, sl] = acc_ref[:, sl] / l_ref[h]


def _flash(q, ktb, vtb, c, nb, seq, tq, tk):
    m, w = q.shape
    nq, nk = seq // tq, seq // tk
    kern = functools.partial(_flash_kernel, tq=tq, tk=tk)
    return pl.pallas_call(
        kern,
        grid=(nb, nq, nk),
        in_specs=[
            pl.BlockSpec((tq, w), lambda b, i, j: (b * nq + i, 0)),
            pl.BlockSpec((1, w, tk), lambda b, i, j: (b, 0, jnp.minimum(j, i))),
            pl.BlockSpec((1, w, tk), lambda b, i, j: (b, 0, jnp.minimum(j, i))),
            pl.BlockSpec((1, ATTN_HEADS, tq), lambda b, i, j: (b, 0, i)),
            pl.BlockSpec((1, ATTN_HEADS, tk), lambda b, i, j: (b, 0, jnp.minimum(j, i))),
        ],
        out_specs=pl.BlockSpec((tq, w), lambda b, i, j: (b * nq + i, 0)),
        out_shape=jax.ShapeDtypeStruct((m, w), F32),
        scratch_shapes=[
            pltpu.VMEM((ATTN_HEADS, tq, 1), F32),
            pltpu.VMEM((ATTN_HEADS, tq, 1), F32),
            pltpu.VMEM((tq, w), F32),
            pltpu.VMEM((tq, LANES), F32),
        ],
        compiler_params=_cparams(("parallel", "parallel", "arbitrary")),
        name="fox_flash",
    )(q, ktb, vtb, c, c)


def _decode_kernel(pt_ref, q_ref, kn_ref, vn_ref, lfn_ref, *refs, pps, lq):
    del pt_ref
    k_refs = refs[0:pps]
    v_refs = refs[pps:2 * pps]
    lf_refs = refs[2 * pps:3 * pps]
    o_ref, qbd_ref, m_ref, l_ref, acc_ref, carry_ref = refs[3 * pps:]
    j = pl.program_id(1)
    nrow = ATTN_HEADS * lq
    hd = ATTN_HEAD_DIM

    @pl.when(j == 0)
    def _():
        qt = jnp.concatenate([q_ref[0] * (hd ** -0.5)] * ATTN_HEADS, axis=0)
        rh = lax.broadcasted_iota(jnp.int32, (nrow, ATTN_WIDTH), 0) // lq
        ch = lax.broadcasted_iota(jnp.int32, (nrow, ATTN_WIDTH), 1) // hd
        qbd_ref[...] = jnp.where(rh == ch, qt, 0.0).astype(BF16)
        m_ref[...] = jnp.full(m_ref.shape, NEG_BIG, F32)
        l_ref[...] = jnp.zeros(l_ref.shape, F32)
        acc_ref[...] = jnp.zeros(acc_ref.shape, F32)
        carry_ref[...] = jnp.zeros(carry_ref.shape, F32)

    def expand_heads(r):
        return jnp.concatenate(
            [jnp.broadcast_to(r[h:h + 1, :], (lq, r.shape[1])) for h in range(ATTN_HEADS)], axis=0)

    def merge(s_list, v_list, nt):
        m_prev = m_ref[...]
        m_new = m_prev
        for s in s_list:
            m_new = jnp.maximum(m_new, jnp.max(s, axis=1, keepdims=True))
        alpha = jnp.exp(m_prev - m_new)
        lsum = jnp.zeros_like(m_prev)
        pv = None
        for s, v in zip(s_list, v_list):
            p = jnp.exp(s - m_new)
            lsum = lsum + jnp.sum(p, axis=1, keepdims=True)
            if nt:
                d = lax.dot_general(p.astype(BF16), v, _NT, preferred_element_type=F32)
            else:
                d = jnp.dot(p.astype(BF16), v, preferred_element_type=F32)
            pv = d if pv is None else pv + d
        l_ref[...] = alpha * l_ref[...] + lsum
        acc_ref[...] = alpha * acc_ref[...] + pv
        m_ref[...] = m_new

    later = _tri(PAGE_SIZE, lambda r, c: r > c)
    carry = carry_ref[...]
    s_list, v_list = [], []
    for p in range(pps):
        lf = lf_refs[p][0]
        r = _dot3(lf, later) + carry
        carry = carry + jnp.sum(lf, axis=1, keepdims=True)
        s = jnp.dot(qbd_ref[...], k_refs[p][0].astype(BF16), preferred_element_type=F32)
        s_list.append(s + expand_heads(r))
        v_list.append(v_refs[p][0].astype(BF16))
    carry_ref[...] = carry
    merge(s_list, v_list, nt=True)

    @pl.when(j == pl.num_programs(1) - 1)
    def _():
        pad = jnp.zeros((LANES - lq, ATTN_WIDTH), F32)
        kn = jnp.concatenate([kn_ref[0], pad], axis=0).astype(BF16)
        vn = jnp.concatenate([vn_ref[0], pad], axis=0).astype(BF16)
        s = lax.dot_general(qbd_ref[...], kn, _NT, preferred_element_type=F32)
        cn = _dot3(lfn_ref[0], _tri(LANES, lambda r, c: r <= c))
        s = s - expand_heads(cn)
        qpos = lax.broadcasted_iota(jnp.int32, (nrow, LANES), 0) % lq
        kpos = lax.broadcasted_iota(jnp.int32, (nrow, LANES), 1)
        s = jnp.where(kpos <= qpos, s, NEG_BIG)
        merge([s], [vn], nt=False)
        out = acc_ref[...] / l_ref[...]
        for h in range(ATTN_HEADS):
            o_ref[0, :, h * hd:(h + 1) * hd] = out[h * lq:(h + 1) * lq, h * hd:(h + 1) * hd]


def _decode(q, kn, vn, lfn_t, ck, cv, clf, page_table, pps):
    nb, lq, w = q.shape
    npages = page_table.shape[1]
    nsteps = npages // pps
    nrow = ATTN_HEADS * lq
    kern = functools.partial(_decode_kernel, pps=pps, lq=lq)

    def page_spec(shape, p):
        return pl.BlockSpec(shape, lambda b, j, pt: (pt[b, npages - 1 - (j * pps + p)], 0, 0))

    tokspec = lambda: pl.BlockSpec((1, lq, w), lambda b, j, pt: (b, 0, 0))
    in_specs = [tokspec(), tokspec(), tokspec(),
                pl.BlockSpec((1, ATTN_HEADS, LANES), lambda b, j, pt: (b, 0, 0))]
    in_specs += [page_spec((1, w, PAGE_SIZE), p) for p in range(pps)]
    in_specs += [page_spec((1, w, PAGE_SIZE), p) for p in range(pps)]
    in_specs += [page_spec((1, ATTN_HEADS, PAGE_SIZE), p) for p in range(pps)]
    return pl.pallas_call(
        kern,
        grid_spec=pltpu.PrefetchScalarGridSpec(
            num_scalar_prefetch=1,
            grid=(nb, nsteps),
            in_specs=in_specs,
            out_specs=pl.BlockSpec((1, lq, w), lambda b, j, pt: (b, 0, 0)),
            scratch_shapes=[
                pltpu.VMEM((nrow, w), BF16),
                pltpu.VMEM((nrow, 1), F32),
                pltpu.VMEM((nrow, 1), F32),
                pltpu.VMEM((nrow, w), F32),
                pltpu.VMEM((ATTN_HEADS, 1), F32),
            ],
        ),
        out_shape=jax.ShapeDtypeStruct((nb, lq, w), F32),
        compiler_params=_cparams(("parallel", "arbitrary")),
        name="fox_decode",
    )(page_table, q, kn, vn, lfn_t, *([ck] * pps), *([cv] * pps), *([clf] * pps))


def _outproj1_kernel(o_ref, z_ref, w_ref, np_ref, x_ref, out_ref):
    g = (o_ref[...] * _silu(z_ref[...])).astype(BF16)
    y = jnp.dot(g, w_ref[...], preferred_element_type=F32)
    out_ref[...] = x_ref[...] + _rms(y) * np_ref[...]


def _outproj1(o, z, zblk, w, npost, x, tm):
    m = o.shape[0]
    return pl.pallas_call(
        _outproj1_kernel,
        grid=(m // tm,),
        in_specs=[
            pl.BlockSpec((tm, ATTN_WIDTH), lambda i: (i, 0)),
            pl.BlockSpec((tm, ATTN_WIDTH), lambda i: (i, zblk)),
            pl.BlockSpec((ATTN_WIDTH, D_MODEL), lambda i: (0, 0)),
            pl.BlockSpec((1, D_MODEL), lambda i: (0, 0)),
            pl.BlockSpec((tm, D_MODEL), lambda i: (i, 0)),
        ],
        out_specs=pl.BlockSpec((tm, D_MODEL), lambda i: (i, 0)),
        out_shape=jax.ShapeDtypeStruct((m, D_MODEL), F32),
        compiler_params=_cparams(("parallel",)),
        name="outproj1",
    )(o, z, w, npost, x)


def _pad_lanes(v):
    return jnp.pad(v.astype(F32), (0, LANES - v.shape[0])).reshape(1, LANES)


def kernel(x_prompt, x_sample, state_ssm, state_conv, cache_k, cache_v, cache_logf, page_table, norm_pre, norm_post, ssm_w_in, ssm_conv_w, ssm_conv_b, ssm_dt_bias, ssm_a_log, ssm_d, ssm_norm_w, ssm_w_out, attn_w_in, attn_b_f, attn_w_out):
    nbp, seq, d = x_prompt.shape
    nbs, lq, _ = x_sample.shape
    xp = x_prompt.reshape(nbp * seq, d)
    xs = x_sample.reshape(nbs * lq, d)
    tm_p = min(1024, seq)

    w_in = ssm_w_in[0]
    z_end = SSM_D_INNER
    xbc_end = SSM_D_INNER + SSM_CONV_DIM
    wm = jnp.concatenate([w_in[:, z_end:xbc_end], w_in[:, :z_end]], axis=1).astype(BF16)
    wd = jnp.pad(w_in[:, xbc_end:], ((0, 0), (0, LANES - SSM_HEADS))).astype(BF16)
    g0 = norm_pre[0:1]
    ssd_vecs = (ssm_conv_w[0], ssm_conv_b[0:1], _pad_lanes(ssm_dt_bias[0]), _pad_lanes(ssm_a_log[0]),
                _pad_lanes(ssm_d[0]))
    w_out0 = ssm_w_out[0].astype(BF16)
    nw0 = ssm_norm_w[0:1]
    np0 = norm_post[0:1]

    om_p, od_p = _inproj0(xp, g0, wm, wd, tm_p)
    om_s, od_s = _inproj0(xs, g0, wm, wd, nbs * lq)

    conv0_p = jnp.zeros((nbp, SSM_CONV - 1, SSM_CONV_DIM), F32)
    h0_p = jnp.zeros((nbp, SSM_HEADS, SSM_HEAD_DIM, SSM_STATE), F32)
    y_p, ssm_p, conv_p = _ssd(om_p, od_p, *ssd_vecs, conv0_p, h0_p, nbp, seq // SSM_CHUNK, SSM_CHUNK, SSM_CHUNK)
    y_s, ssm_s, conv_s = _ssd(om_s, od_s, *ssd_vecs, state_conv[0], state_ssm[0], nbs, 1, lq, lq)

    hp1 = _outproj0(y_p, om_p, nw0, w_out0, np0, xp, min(512, seq))
    hs1 = _outproj0(y_s, om_s, nw0, w_out0, np0, xs, nbs * lq)

    wa = attn_w_in[0]
    w = ATTN_WIDTH
    wa_b = wa[:, :4 * w].astype(BF16)
    wa_t = wa.T.astype(BF16)
    wf = jnp.pad(wa[:, 4 * w:], ((0, 0), (0, LANES - ATTN_HEADS))).astype(BF16)
    g1 = norm_pre[1:2]
    np1 = norm_post[1:2]
    w_out1 = attn_w_out[0].astype(BF16)

    q_p, kt, ktb, vt, vtb, z_p, lft = _inproj1p(
        hp1, g1, wa_b[:, :w], wa_t[w:2 * w], wa_t[2 * w:3 * w], wa_b[:, 3 * w:4 * w], wa_t[4 * w:],
        attn_b_f[0].astype(F32).reshape(ATTN_HEADS, 1), nbp, seq, min(512, seq))
    c_p = _cumsum(lft)
    tq = min(512, seq)
    o_p = _flash(q_p, ktb, vtb, c_p, nbp, seq, tq, tq)
    y_prompt = _outproj1(o_p, z_p, 0, w_out1, np1, hp1, min(512, seq))

    proj_s, lf_s = _inproj1s(hs1, g1, wa_b, wf, _pad_lanes(attn_b_f[0]))
    q_s = proj_s[:, :w].reshape(nbs, lq, w)
    k_s = proj_s[:, w:2 * w].reshape(nbs, lq, w)
    v_s = proj_s[:, 2 * w:3 * w].reshape(nbs, lq, w)
    lf_s = lf_s[:, :ATTN_HEADS].reshape(nbs, lq, ATTN_HEADS)
    lfn_t = jnp.pad(lf_s.transpose(0, 2, 1), ((0, 0), (0, 0), (0, LANES - lq)))
    npool = cache_k.shape[1]
    ck = cache_k[0].transpose(0, 2, 3, 1).reshape(npool, w, PAGE_SIZE)
    cv = cache_v[0].transpose(0, 2, 3, 1).reshape(npool, w, PAGE_SIZE)
    clf = cache_logf[0].transpose(0, 2, 1)
    o_s = _decode(q_s, k_s, v_s, lfn_t, ck, cv, clf, page_table, min(4, page_table.shape[1]))
    y_sample = _outproj1(o_s.reshape(nbs * lq, w), proj_s, 3, w_out1, np1, hs1, nbs * lq)

    hd = ATTN_HEAD_DIM
    return (
        y_prompt.reshape(nbp, seq, d),
        y_sample.reshape(nbs, lq, d),
        kt.reshape(nbp, ATTN_HEADS, hd, seq).transpose(0, 3, 1, 2)[None],
        vt.reshape(nbp, ATTN_HEADS, hd, seq).transpose(0, 3, 1, 2)[None],
        lft.transpose(0, 2, 1)[None],
        k_s.reshape(1, nbs, lq, ATTN_HEADS, hd),
        v_s.reshape(1, nbs, lq, ATTN_HEADS, hd),
        lf_s[None],
        ssm_p[None],
        conv_p[None],
        ssm_s[None],
        conv_s[None],
    )
```

```python
import functools

import jax
import jax.numpy as jnp
from jax import lax
from jax.experimental import pallas as pl
from jax.experimental.pallas import tpu as pltpu

F32 = jnp.float32
BF16 = jnp.bfloat16

D_MODEL = 1024
SSM_D_INNER = 2048
SSM_HEAD_DIM = 64
SSM_HEADS = 32
SSM_GROUPS = 8
SSM_HPG = SSM_HEADS // SSM_GROUPS
SSM_STATE = 128
SSM_CONV = 4
SSM_CONV_DIM = SSM_D_INNER + 2 * SSM_GROUPS * SSM_STATE
SSM_CHUNK = 128
ATTN_HEADS = 16
ATTN_HEAD_DIM = 64
ATTN_WIDTH = ATTN_HEADS * ATTN_HEAD_DIM
PAGE_SIZE = 128
RMS_EPS = 1e-6
LANES = 128
SUBLANES = 8
NEG_BIG = -1e30
LOG2E = 1.4426950408889634
AUG = 16
DECODE_PAGES_PER_STEP = 8
VMEM_LIMIT = 56 * 1024 * 1024

_NT = (((1,), (1,)), ((), ()))
_TN = (((0,), (0,)), ((), ()))


def _silu(x):
    return x / (1.0 + jnp.exp(-x))


def _softplus(x):
    return jnp.maximum(x, 0.0) + jnp.log1p(jnp.exp(-jnp.abs(x)))


def _log_sigmoid(x):
    return jnp.minimum(x, 0.0) - jnp.log1p(jnp.exp(-jnp.abs(x)))


def _rms(x):
    return x * lax.rsqrt(jnp.mean(x * x, axis=-1, keepdims=True) + RMS_EPS)


def _split3(x):
    hi = x.astype(BF16)
    r1 = x - hi.astype(F32)
    mid = r1.astype(BF16)
    lo = (r1 - mid.astype(F32)).astype(BF16)
    return hi, mid, lo


def _pieces2(x):
    hi = x.astype(BF16)
    return hi, (x - hi.astype(F32)).astype(BF16)


def _dot3(x, tri):
    n = x.shape[0]
    hi, mid, lo = _split3(x)
    r = jnp.dot(jnp.concatenate([hi, mid, lo], axis=0), tri, preferred_element_type=F32)
    return r[:n] + r[n:2 * n] + r[2 * n:]


def _tri(n, fn):
    r = lax.broadcasted_iota(jnp.int32, (n, n), 0)
    c = lax.broadcasted_iota(jnp.int32, (n, n), 1)
    return jnp.where(fn(r, c), 1.0, 0.0).astype(BF16)


def _cparams(sem, **kw):
    return pltpu.CompilerParams(dimension_semantics=sem, vmem_limit_bytes=VMEM_LIMIT, **kw)


def _inproj0_kernel(x_ref, g_ref, wm_ref, wd_ref, om_ref, od_ref, un_ref):
    @pl.when(pl.program_id(1) == 0)
    def _():
        un = (_rms(x_ref[...]) * g_ref[...]).astype(BF16)
        un_ref[...] = un
        od_ref[...] = jnp.dot(un, wd_ref[...], preferred_element_type=F32)

    om_ref[...] = jnp.dot(un_ref[...], wm_ref[...], preferred_element_type=F32)


def _inproj0(x, g, wm, wd, tm, tn=1024):
    m, d = x.shape
    n = wm.shape[1]
    return pl.pallas_call(
        _inproj0_kernel,
        grid=(m // tm, n // tn),
        in_specs=[
            pl.BlockSpec((tm, d), lambda i, j: (i, 0)),
            pl.BlockSpec((1, d), lambda i, j: (0, 0)),
            pl.BlockSpec((d, tn), lambda i, j: (0, j)),
            pl.BlockSpec((d, LANES), lambda i, j: (0, 0)),
        ],
        out_specs=[
            pl.BlockSpec((tm, tn), lambda i, j: (i, j)),
            pl.BlockSpec((tm, LANES), lambda i, j: (i, 0)),
        ],
        out_shape=[jax.ShapeDtypeStruct((m, n), F32), jax.ShapeDtypeStruct((m, LANES), F32)],
        scratch_shapes=[pltpu.VMEM((tm, d), BF16)],
        compiler_params=_cparams(("parallel", "arbitrary")),
        name="inproj0",
    )(x, g, wm, wd)


def _ssd_kernel(xbc_ref, dt_ref, cw_ref, cb_ref, dtb_ref, alog_ref, dexp_ref, e_ref, conv0_ref, h0_ref,
                y_ref, hout_ref, convout_ref,
                xpad_ref, act_ref, ht_ref, act_t_ref, ex_ref, *, rows_in, n_valid):
    q = SSM_CHUNK
    gw = SSM_HPG * SSM_HEAD_DIM
    c = pl.program_id(1)
    last_c = pl.num_programs(1) - 1
    halo = SUBLANES
    tail0 = halo - (SSM_CONV - 1)

    @pl.when(c == 0)
    def _():
        xpad_ref[0:halo, :] = jnp.zeros((halo, SSM_CONV_DIM), F32)
        xpad_ref[tail0:halo, :] = conv0_ref[0]
        if rows_in < q:
            xpad_ref[halo + rows_in:halo + q, :] = jnp.zeros((q - rows_in, SSM_CONV_DIM), F32)
        for g in range(SSM_GROUPS):
            ht_ref[g] = h0_ref[0, g * SSM_HPG:(g + 1) * SSM_HPG].reshape(gw, SSM_STATE).T

    xpad_ref[halo:halo + rows_in, :] = xbc_ref[...]

    cchunk = 512
    for j0 in range(0, SSM_CONV_DIM, cchunk):
        cs = slice(j0, j0 + cchunk)
        sh = xpad_ref[:, cs]
        acc = cb_ref[:, cs] + sh[halo:] * cw_ref[SSM_CONV - 1:SSM_CONV, cs]
        for k in range(SSM_CONV - 2, -1, -1):
            sh = pltpu.roll(sh, 1, axis=0)
            acc = acc + sh[halo:] * cw_ref[k:k + 1, cs]
        act_ref[:, cs] = _silu(acc)

    new_tail = xpad_ref[halo + n_valid - (SSM_CONV - 1):halo + n_valid, :]
    xpad_ref[tail0:halo, :] = new_tail

    @pl.when(c == last_c)
    def _():
        convout_ref[0] = new_tail

    row = lax.broadcasted_iota(jnp.int32, (q, LANES), 0)
    dt_raw = dt_ref[...]
    if rows_in < q:
        dt_raw = jnp.concatenate([dt_raw, jnp.zeros((q - rows_in, LANES), F32)], axis=0)
    dt = _softplus(dt_raw + dtb_ref[...])
    if n_valid < q:
        dt = jnp.where(row < n_valid, dt, 0.0)
    adt = dt * (-jnp.exp(alog_ref[...]))
    acum = adt
    k = 1
    while k < q:
        acum = acum + jnp.where(row >= k, pltpu.roll(acum, k, axis=0), 0.0)
        k *= 2
    act_t_ref[...] = acum.T
    eac = jnp.exp(acum)
    dec = jnp.exp(acum[q - 1:q, :] - acum)
    for i, val in enumerate((dt, dt * dec, eac)):
        hi, mid = _pieces2(val)
        ex = jnp.dot(jnp.concatenate([hi, mid], axis=0), e_ref[...], preferred_element_type=F32)
        ex_ref[i] = ex[:q] + ex[q:]

    r_i = lax.broadcasted_iota(jnp.int32, (q, q), 0)
    c_i = lax.broadcasted_iota(jnp.int32, (q, q), 1)
    causal = r_i >= c_i
    blk_r = lax.broadcasted_iota(jnp.int32, (SSM_HPG * q, gw), 0) // q
    blk_c = lax.broadcasted_iota(jnp.int32, (SSM_HPG * q, gw), 1) // SSM_HEAD_DIM
    diag_blocks = blk_r == blk_c

    gn = SSM_GROUPS * SSM_STATE
    for g in range(SSM_GROUPS):
        gs = slice(g * gw, (g + 1) * gw)
        bt = act_ref[:, SSM_D_INNER + g * SSM_STATE:SSM_D_INNER + (g + 1) * SSM_STATE].T.astype(BF16)
        c_g = act_ref[:, SSM_D_INNER + gn + g * SSM_STATE:SSM_D_INNER + gn + (g + 1) * SSM_STATE].astype(BF16)
        cb = jnp.dot(c_g, bt, preferred_element_type=F32)
        mats = []
        for r in range(SSM_HPG):
            h = g * SSM_HPG + r
            seg = acum[:, h:h + 1] - act_t_ref[h:h + 1, :]
            mats.append((cb * jnp.exp(jnp.where(causal, seg, -jnp.inf))).astype(BF16))
        mcat = jnp.concatenate(mats, axis=1)
        xs_g = act_ref[:, gs]
        xdt = (xs_g * ex_ref[0, :, gs]).astype(BF16)
        bd = jnp.where(diag_blocks, jnp.concatenate([xdt] * SSM_HPG, axis=0), jnp.zeros((), BF16))
        y_diag = jnp.dot(mcat, bd, preferred_element_type=F32)
        ht = ht_ref[g]
        y_off = jnp.dot(c_g, ht.astype(BF16), preferred_element_type=F32)
        eac_g = ex_ref[2, :, gs]
        y = y_diag + y_off * eac_g + xs_g * dexp_ref[:, gs]
        y_ref[:, gs] = y[:rows_in]
        xdtd = (xs_g * ex_ref[1, :, gs]).astype(BF16)
        ht_ref[g] = ht * eac_g[q - 1:q, :] + jnp.dot(bt, xdtd, preferred_element_type=F32)

    @pl.when(c == last_c)
    def _():
        for g in range(SSM_GROUPS):
            hout_ref[0, g * SSM_HPG:(g + 1) * SSM_HPG] = ht_ref[g].T.reshape(SSM_HPG, SSM_HEAD_DIM, SSM_STATE)


def _ssd(om, od, cw, cb, dtb, alog, dexp, emat, conv0, h0, nb, nc, rows_in, n_valid):
    m = om.shape[0]
    gw = SSM_HPG * SSM_HEAD_DIM
    kern = functools.partial(_ssd_kernel, rows_in=rows_in, n_valid=n_valid)
    vec = lambda: pl.BlockSpec((1, LANES), lambda b, c: (0, 0))
    return pl.pallas_call(
        kern,
        grid=(nb, nc),
        in_specs=[
            pl.BlockSpec((rows_in, SSM_CONV_DIM), lambda b, c: (b * nc + c, 0)),
            pl.BlockSpec((rows_in, LANES), lambda b, c: (b * nc + c, 0)),
            pl.BlockSpec((SSM_CONV, SSM_CONV_DIM), lambda b, c: (0, 0)),
            pl.BlockSpec((1, SSM_CONV_DIM), lambda b, c: (0, 0)),
            vec(), vec(),
            pl.BlockSpec((1, SSM_D_INNER), lambda b, c: (0, 0)),
            pl.BlockSpec((LANES, SSM_D_INNER), lambda b, c: (0, 0)),
            pl.BlockSpec((1, SSM_CONV - 1, SSM_CONV_DIM), lambda b, c: (b, 0, 0)),
            pl.BlockSpec((1, SSM_HEADS, SSM_HEAD_DIM, SSM_STATE), lambda b, c: (b, 0, 0, 0)),
        ],
        out_specs=[
            pl.BlockSpec((rows_in, SSM_D_INNER), lambda b, c: (b * nc + c, 0)),
            pl.BlockSpec((1, SSM_HEADS, SSM_HEAD_DIM, SSM_STATE), lambda b, c: (b, 0, 0, 0)),
            pl.BlockSpec((1, SSM_CONV - 1, SSM_CONV_DIM), lambda b, c: (b, 0, 0)),
        ],
        out_shape=[
            jax.ShapeDtypeStruct((m, SSM_D_INNER), F32),
            jax.ShapeDtypeStruct((nb, SSM_HEADS, SSM_HEAD_DIM, SSM_STATE), F32),
            jax.ShapeDtypeStruct((nb, SSM_CONV - 1, SSM_CONV_DIM), F32),
        ],
        scratch_shapes=[
            pltpu.VMEM((SUBLANES + SSM_CHUNK, SSM_CONV_DIM), F32),
            pltpu.VMEM((SSM_CHUNK, SSM_CONV_DIM), F32),
            pltpu.VMEM((SSM_GROUPS, SSM_STATE, gw), F32),
            pltpu.VMEM((LANES, SSM_CHUNK), F32),
            pltpu.VMEM((3, SSM_CHUNK, SSM_D_INNER), F32),
        ],
        compiler_params=_cparams(("arbitrary", "arbitrary")),
        name="ssd",
    )(om, od, cw, cb, dtb, alog, dexp, emat, conv0, h0)


def _outproj0_kernel(y_ref, z_ref, nw_ref, w_ref, np_ref, x_ref, o_ref):
    h = y_ref[...] * _silu(z_ref[...])
    gw = SSM_D_INNER // SSM_GROUPS
    hn = jnp.concatenate([_rms(h[:, g * gw:(g + 1) * gw]) for g in range(SSM_GROUPS)], axis=1)
    hn = (hn * nw_ref[...]).astype(BF16)
    o = jnp.dot(hn, w_ref[...], preferred_element_type=F32)
    o_ref[...] = x_ref[...] + _rms(o) * np_ref[...]


def _outproj0(y, om, nw, w, npost, x, tm):
    m = y.shape[0]
    zblk = SSM_CONV_DIM // SSM_D_INNER
    return pl.pallas_call(
        _outproj0_kernel,
        grid=(m // tm,),
        in_specs=[
            pl.BlockSpec((tm, SSM_D_INNER), lambda i: (i, 0)),
            pl.BlockSpec((tm, SSM_D_INNER), lambda i: (i, zblk)),
            pl.BlockSpec((1, SSM_D_INNER), lambda i: (0, 0)),
            pl.BlockSpec((SSM_D_INNER, D_MODEL), lambda i: (0, 0)),
            pl.BlockSpec((1, D_MODEL), lambda i: (0, 0)),
            pl.BlockSpec((tm, D_MODEL), lambda i: (i, 0)),
        ],
        out_specs=pl.BlockSpec((tm, D_MODEL), lambda i: (i, 0)),
        out_shape=jax.ShapeDtypeStruct((m, D_MODEL), F32),
        compiler_params=_cparams(("parallel",)),
        name="outproj0",
    )(y, om, nw, w, npost, x)


def _inproj1p_kernel(x_ref, g_ref, wt_ref, wft_ref, bf_ref, qt_ref, kt_ref, vt_ref, vtb_ref, zt_ref, lft_ref):
    un = (_rms(x_ref[...]) * g_ref[...]).astype(BF16)
    w = ATTN_WIDTH
    proj = lambda i: lax.dot_general(wt_ref[i * w:(i + 1) * w, :], un, _NT, preferred_element_type=F32)
    qt_ref[0] = (proj(0) * (ATTN_HEAD_DIM ** -0.5 * LOG2E)).astype(BF16)
    kt_ref[0] = proj(1)
    vt = proj(2)
    vt_ref[0] = vt
    vtb_ref[0] = vt.astype(BF16)
    zt_ref[0] = proj(3)
    ft = lax.dot_general(wft_ref[...], un, _NT, preferred_element_type=F32)
    lft_ref[0] = _log_sigmoid(ft + bf_ref[...])


def _inproj1p(x, g, wt, wft, bfc, nb, seq, tm):
    m, d = x.shape
    nq = seq // tm
    w = ATTN_WIDTH
    full = lambda r, c: pl.BlockSpec((r, c), lambda b, i: (0, 0))
    feat = lambda r: pl.BlockSpec((1, r, tm), lambda b, i: (b, 0, i))
    fshape = lambda dt: jax.ShapeDtypeStruct((nb, w, seq), dt)
    return pl.pallas_call(
        _inproj1p_kernel,
        grid=(nb, nq),
        in_specs=[pl.BlockSpec((tm, d), lambda b, i: (b * nq + i, 0)), full(1, d), full(4 * w, d),
                  full(ATTN_HEADS, d), full(ATTN_HEADS, 1)],
        out_specs=[feat(w), feat(w), feat(w), feat(w), feat(w), feat(ATTN_HEADS)],
        out_shape=[fshape(BF16), fshape(F32), fshape(F32), fshape(BF16), fshape(F32),
                   jax.ShapeDtypeStruct((nb, ATTN_HEADS, seq), F32)],
        compiler_params=_cparams(("parallel", "parallel")),
        name="inproj1p",
    )(x, g, wt, wft, bfc)


def _inproj1s_kernel(x_ref, g_ref, w_ref, wf_ref, bf_ref, o_ref, lf_ref):
    un = (_rms(x_ref[...]) * g_ref[...]).astype(BF16)
    o_ref[...] = jnp.dot(un, w_ref[...], preferred_element_type=F32)
    lf_ref[...] = _log_sigmoid(jnp.dot(un, wf_ref[...], preferred_element_type=F32) + bf_ref[...])


def _inproj1s(x, g, w, wf, bfr):
    m, d = x.shape
    n = w.shape[1]
    return pl.pallas_call(
        _inproj1s_kernel,
        grid=(1,),
        in_specs=[pl.BlockSpec((m, d), lambda i: (0, 0)), pl.BlockSpec((1, d), lambda i: (0, 0)),
                  pl.BlockSpec((d, n), lambda i: (0, 0)), pl.BlockSpec((d, LANES), lambda i: (0, 0)),
                  pl.BlockSpec((1, LANES), lambda i: (0, 0))],
        out_specs=[pl.BlockSpec((m, n), lambda i: (0, 0)), pl.BlockSpec((m, LANES), lambda i: (0, 0))],
        out_shape=[jax.ShapeDtypeStruct((m, n), F32), jax.ShapeDtypeStruct((m, LANES), F32)],
        compiler_params=_cparams(("arbitrary",)),
        name="inproj1s",
    )(x, g, w, wf, bfr)


def _cumsum_kernel(x_ref, o_ref):
    nblk = x_ref.shape[2] // LANES
    upper = _tri(LANES, lambda r, c: r <= c)

    def body(j, carry):
        off = pl.multiple_of(j * LANES, LANES)
        loc = _dot3(x_ref[0, :, pl.ds(off, LANES)], upper)
        o_ref[0, :, pl.ds(off, LANES)] = loc + carry
        return carry + loc[:, LANES - 1:LANES]

    lax.fori_loop(0, nblk, body, jnp.zeros((x_ref.shape[1], 1), F32))


def _cumsum(x):
    nb, h, seq = x.shape
    return pl.pallas_call(
        _cumsum_kernel,
        grid=(nb,),
        in_specs=[pl.BlockSpec((1, h, seq), lambda b: (b, 0, 0))],
        out_specs=pl.BlockSpec((1, h, seq), lambda b: (b, 0, 0)),
        out_shape=jax.ShapeDtypeStruct((nb, h, seq), F32),
        compiler_params=_cparams(("parallel",)),
        name="lf_cumsum",
    )(x)


def _bias_rows(c2, first, n, width):
    hi, mid, lo = (p.astype(F32) for p in _split3(c2))
    row = lax.broadcasted_iota(jnp.int32, (n, width), 0)
    ones_first = 3 - first
    out = jnp.where((row >= ones_first) & (row < ones_first + 3), 1.0, 0.0)
    for i, piece in enumerate((hi, mid, lo)):
        out = jnp.where(row == first + i, piece, out)
    return out


def _kaug_kernel(kt_ref, c_ref, o_ref):
    h = pl.program_id(1)
    tk = kt_ref.shape[2]
    c2 = c_ref[0, pl.ds(h, 1), :] * LOG2E
    full = jnp.concatenate([kt_ref[0], _bias_rows(-c2, 3, ATTN_HEAD_DIM, tk)], axis=0)
    o_ref[0, 0] = full.T.astype(BF16)


def _kaug(kt, c, tk):
    nb, w, seq = kt.shape
    return pl.pallas_call(
        _kaug_kernel,
        grid=(nb, ATTN_HEADS, seq // tk),
        in_specs=[pl.BlockSpec((1, ATTN_HEAD_DIM, tk), lambda b, h, j: (b, h, j)),
                  pl.BlockSpec((1, ATTN_HEADS, tk), lambda b, h, j: (b, 0, j))],
        out_specs=pl.BlockSpec((1, 1, tk, LANES), lambda b, h, j: (b, h, j, 0)),
        out_shape=jax.ShapeDtypeStruct((nb, ATTN_HEADS, seq, LANES), BF16),
        compiler_params=_cparams(("parallel", "parallel", "parallel")),
        name="fox_kaug",
    )(kt, c)


def _flash_kernel(qt_ref, ka_ref, vt_ref, cq_ref, o_ref, qa_ref, m_ref, acc_ref, s_ref, *, tq, tk):
    qi = pl.program_id(1)
    kj = pl.program_id(2)
    hd = ATTN_HEAD_DIM

    @pl.when(kj == 0)
    def _():
        m_ref[...] = jnp.full(m_ref.shape, NEG_BIG, F32)
        acc_ref[...] = jnp.zeros(acc_ref.shape, F32)

        def build(h, carry):
            off = pl.multiple_of(h * hd, hd)
            c2 = cq_ref[0, pl.ds(h, 1), :] * LOG2E
            qa_ref[h, 0:hd, :] = qt_ref[0, pl.ds(off, hd), :]
            qa_ref[h, hd:2 * hd, :] = _bias_rows(c2, 0, hd, tq).astype(BF16)
            return carry

        lax.fori_loop(0, ATTN_HEADS, build, 0)

    def step(masked):
        if masked:
            key = kj * tk + lax.broadcasted_iota(jnp.int32, (tk, tq), 0)
            qry = qi * tq + lax.broadcasted_iota(jnp.int32, (tk, tq), 1)
            keep = key <= qry
        ones = jnp.ones((AUG, tk), BF16)

        def scores(h, slot):
            st = jnp.dot(ka_ref[0, h], qa_ref[h], preferred_element_type=F32)
            if masked:
                st = jnp.where(keep, st, NEG_BIG)
            s_ref[slot] = st
            return jnp.max(st, axis=0, keepdims=True)

        def softmax_pv(h, slot, mx):
            off = pl.multiple_of(h * hd, hd)
            m_prev = m_ref[h]
            m_new = jnp.maximum(m_prev, mx)
            pt = jnp.exp2(s_ref[slot] - m_new[0:1]).astype(BF16)
            va = jnp.concatenate([vt_ref[0, pl.ds(off, hd), :], ones], axis=0)
            pv = jnp.dot(va, pt, preferred_element_type=F32)
            acc_ref[h] = acc_ref[h] * jnp.exp2(m_prev - m_new)[0:1] + pv
            m_ref[h] = m_new

        def body(i, mx0):
            h0 = 2 * i
            mx1 = scores(h0 + 1, 1)
            softmax_pv(h0, 0, mx0)
            mx0 = scores(h0 + 2, 0)
            softmax_pv(h0 + 1, 1, mx1)
            return mx0

        mx0 = lax.fori_loop(0, ATTN_HEADS // 2 - 1, body, scores(0, 0))
        mx1 = scores(ATTN_HEADS - 1, 1)
        softmax_pv(ATTN_HEADS - 2, 0, mx0)
        softmax_pv(ATTN_HEADS - 1, 1, mx1)

    @pl.when(kj < qi)
    def _():
        step(False)

    @pl.when(kj == qi)
    def _():
        step(True)

        def fin(h, carry):
            off = pl.multiple_of(h * hd, hd)
            a = acc_ref[h]
            o_ref[0, pl.ds(off, hd), :] = a[0:hd] / a[hd:hd + 1]
            return carry

        lax.fori_loop(0, ATTN_HEADS, fin, 0)


def _flash(qt, ka, vtb, c, tq, tk):
    nb, w, seq = qt.shape
    nq, nk = seq // tq, seq // tk
    kern = functools.partial(_flash_kernel, tq=tq, tk=tk)
    return pl.pallas_call(
        kern,
        grid=(nb, nq, nk),
        in_specs=[
            pl.BlockSpec((1, w, tq), lambda b, i, j: (b, 0, i)),
            pl.BlockSpec((1, ATTN_HEADS, tk, LANES), lambda b, i, j: (b, 0, jnp.minimum(j, i), 0)),
            pl.BlockSpec((1, w, tk), lambda b, i, j: (b, 0, jnp.minimum(j, i))),
            pl.BlockSpec((1, ATTN_HEADS, tq), lambda b, i, j: (b, 0, i)),
        ],
        out_specs=pl.BlockSpec((1, w, tq), lambda b, i, j: (b, 0, i)),
        out_shape=jax.ShapeDtypeStruct((nb, w, seq), F32),
        scratch_shapes=[
            pltpu.VMEM((ATTN_HEADS, 2 * ATTN_HEAD_DIM, tq), BF16),
            pltpu.VMEM((ATTN_HEADS, SUBLANES, tq), F32),
            pltpu.VMEM((ATTN_HEADS, ATTN_HEAD_DIM + AUG, tq), F32),
            pltpu.VMEM((2, tk, tq), F32),
        ],
        compiler_params=_cparams(("parallel", "parallel", "arbitrary")),
        name="fox_flash",
    )(qt, ka, vtb, c)


def _decode_kernel(pt_ref, q_ref, kn_ref, vn_ref, lfn_ref, *refs, pps, lq):
    del pt_ref
    k_refs = refs[0:pps]
    v_refs = refs[pps:2 * pps]
    lf_refs = refs[2 * pps:3 * pps]
    o_ref, qbd_ref, m_ref, l_ref, acc_ref, carry_ref = refs[3 * pps:]
    j = pl.program_id(1)
    nrow = ATTN_HEADS * lq
    hd = ATTN_HEAD_DIM

    @pl.when(j == 0)
    def _():
        qt = jnp.concatenate([q_ref[0] * (hd ** -0.5)] * ATTN_HEADS, axis=0)
        rh = lax.broadcasted_iota(jnp.int32, (nrow, ATTN_WIDTH), 0) // lq
        ch = lax.broadcasted_iota(jnp.int32, (nrow, ATTN_WIDTH), 1) // hd
        qbd_ref[...] = jnp.where(rh == ch, qt, 0.0).astype(BF16)
        m_ref[...] = jnp.full(m_ref.shape, NEG_BIG, F32)
        l_ref[...] = jnp.zeros(l_ref.shape, F32)
        acc_ref[...] = jnp.zeros(acc_ref.shape, F32)
        carry_ref[...] = jnp.zeros(carry_ref.shape, F32)

    def expand_heads(r):
        return jnp.concatenate(
            [jnp.broadcast_to(r[h:h + 1, :], (lq, r.shape[1])) for h in range(ATTN_HEADS)], axis=0)

    def merge(s_list, v_list, nt):
        m_prev = m_ref[...]
        m_new = m_prev
        for s in s_list:
            m_new = jnp.maximum(m_new, jnp.max(s, axis=1, keepdims=True))
        alpha = jnp.exp(m_prev - m_new)
        lsum = jnp.zeros_like(m_prev)
        pv = None
        for s, v in zip(s_list, v_list):
            p = jnp.exp(s - m_new)
            lsum = lsum + jnp.sum(p, axis=1, keepdims=True)
            if nt:
                d = lax.dot_general(p.astype(BF16), v, _NT, preferred_element_type=F32)
            else:
                d = jnp.dot(p.astype(BF16), v, preferred_element_type=F32)
            pv = d if pv is None else pv + d
        l_ref[...] = alpha * l_ref[...] + lsum
        acc_ref[...] = alpha * acc_ref[...] + pv
        m_ref[...] = m_new

    later = _tri(PAGE_SIZE, lambda r, c: r > c)
    carry = carry_ref[...]
    s_list, v_list = [], []
    for p in range(pps):
        lf = lf_refs[p][0]
        r = _dot3(lf, later) + carry
        carry = carry + jnp.sum(lf, axis=1, keepdims=True)
        s = jnp.dot(qbd_ref[...], k_refs[p][0].astype(BF16), preferred_element_type=F32)
        s_list.append(s + expand_heads(r))
        v_list.append(v_refs[p][0].astype(BF16))
    carry_ref[...] = carry
    merge(s_list, v_list, nt=True)

    @pl.when(j == pl.num_programs(1) - 1)
    def _():
        pad = jnp.zeros((LANES - lq, ATTN_WIDTH), F32)
        kn = jnp.concatenate([kn_ref[0], pad], axis=0).astype(BF16)
        vn = jnp.concatenate([vn_ref[0], pad], axis=0).astype(BF16)
        s = lax.dot_general(qbd_ref[...], kn, _NT, preferred_element_type=F32)
        cn = _dot3(lfn_ref[0], _tri(LANES, lambda r, c: r <= c))
        s = s - expand_heads(cn)
        qpos = lax.broadcasted_iota(jnp.int32, (nrow, LANES), 0) % lq
        kpos = lax.broadcasted_iota(jnp.int32, (nrow, LANES), 1)
        s = jnp.where(kpos <= qpos, s, NEG_BIG)
        merge([s], [vn], nt=False)
        out = acc_ref[...] / l_ref[...]
        for h in range(ATTN_HEADS):
            o_ref[0, :, h * hd:(h + 1) * hd] = out[h * lq:(h + 1) * lq, h * hd:(h + 1) * hd]


def _decode(q, kn, vn, lfn_t, ck, cv, clf, page_table, pps):
    nb, lq, w = q.shape
    npages = page_table.shape[1]
    nsteps = npages // pps
    nrow = ATTN_HEADS * lq
    kern = functools.partial(_decode_kernel, pps=pps, lq=lq)

    def page_spec(shape, p):
        return pl.BlockSpec(shape, lambda b, j, pt: (pt[b, npages - 1 - (j * pps + p)], 0, 0))

    tokspec = lambda: pl.BlockSpec((1, lq, w), lambda b, j, pt: (b, 0, 0))
    in_specs = [tokspec(), tokspec(), tokspec(),
                pl.BlockSpec((1, ATTN_HEADS, LANES), lambda b, j, pt: (b, 0, 0))]
    in_specs += [page_spec((1, w, PAGE_SIZE), p) for p in range(pps)]
    in_specs += [page_spec((1, w, PAGE_SIZE), p) for p in range(pps)]
    in_specs += [page_spec((1, ATTN_HEADS, PAGE_SIZE), p) for p in range(pps)]
    return pl.pallas_call(
        kern,
        grid_spec=pltpu.PrefetchScalarGridSpec(
            num_scalar_prefetch=1,
            grid=(nb, nsteps),
            in_specs=in_specs,
            out_specs=pl.BlockSpec((1, lq, w), lambda b, j, pt: (b, 0, 0)),
            scratch_shapes=[
                pltpu.VMEM((nrow, w), BF16),
                pltpu.VMEM((nrow, 1), F32),
                pltpu.VMEM((nrow, 1), F32),
                pltpu.VMEM((nrow, w), F32),
                pltpu.VMEM((ATTN_HEADS, 1), F32),
            ],
        ),
        out_shape=jax.ShapeDtypeStruct((nb, lq, w), F32),
        compiler_params=_cparams(("parallel", "arbitrary")),
        name="fox_decode",
    )(page_table, q, kn, vn, lfn_t, *([ck] * pps), *([cv] * pps), *([clf] * pps))


def _outproj1_kernel(o_ref, z_ref, w_ref, np_ref, x_ref, out_ref):
    g = (o_ref[...] * _silu(z_ref[...])).astype(BF16)
    y = jnp.dot(g, w_ref[...], preferred_element_type=F32)
    out_ref[...] = x_ref[...] + _rms(y) * np_ref[...]


def _outproj1(o, z, zblk, w, npost, x, tm):
    m = o.shape[0]
    return pl.pallas_call(
        _outproj1_kernel,
        grid=(m // tm,),
        in_specs=[
            pl.BlockSpec((tm, ATTN_WIDTH), lambda i: (i, 0)),
            pl.BlockSpec((tm, ATTN_WIDTH), lambda i: (i, zblk)),
            pl.BlockSpec((ATTN_WIDTH, D_MODEL), lambda i: (0, 0)),
            pl.BlockSpec((1, D_MODEL), lambda i: (0, 0)),
            pl.BlockSpec((tm, D_MODEL), lambda i: (i, 0)),
        ],
        out_specs=pl.BlockSpec((tm, D_MODEL), lambda i: (i, 0)),
        out_shape=jax.ShapeDtypeStruct((m, D_MODEL), F32),
        compiler_params=_cparams(("parallel",)),
        name="outproj1",
    )(o, z, w, npost, x)


def _outproj1p_kernel(ot_ref, zt_ref, w_ref, np_ref, x_ref, out_ref):
    gt = (ot_ref[0] * _silu(zt_ref[0])).astype(BF16)
    y = lax.dot_general(gt, w_ref[...], _TN, preferred_element_type=F32)
    out_ref[...] = x_ref[...] + _rms(y) * np_ref[...]


def _outproj1p(ot, zt, w, npost, x, tm):
    nb, wd, seq = ot.shape
    nq = seq // tm
    feat = lambda: pl.BlockSpec((1, wd, tm), lambda b, i: (b, 0, i))
    return pl.pallas_call(
        _outproj1p_kernel,
        grid=(nb, nq),
        in_specs=[feat(), feat(),
                  pl.BlockSpec((wd, D_MODEL), lambda b, i: (0, 0)),
                  pl.BlockSpec((1, D_MODEL), lambda b, i: (0, 0)),
                  pl.BlockSpec((tm, D_MODEL), lambda b, i: (b * nq + i, 0))],
        out_specs=pl.BlockSpec((tm, D_MODEL), lambda b, i: (b * nq + i, 0)),
        out_shape=jax.ShapeDtypeStruct((nb * seq, D_MODEL), F32),
        compiler_params=_cparams(("parallel", "parallel")),
        name="outproj1p",
    )(ot, zt, w, npost, x)


def _pad_lanes(v):
    return jnp.pad(v.astype(F32), (0, LANES - v.shape[0])).reshape(1, LANES)


def kernel(x_prompt, x_sample, state_ssm, state_conv, cache_k, cache_v, cache_logf, page_table, norm_pre, norm_post, ssm_w_in, ssm_conv_w, ssm_conv_b, ssm_dt_bias, ssm_a_log, ssm_d, ssm_norm_w, ssm_w_out, attn_w_in, attn_b_f, attn_w_out):
    nbp, seq, d = x_prompt.shape
    nbs, lq, _ = x_sample.shape
    xp = x_prompt.reshape(nbp * seq, d)
    xs = x_sample.reshape(nbs * lq, d)
    tm_p = min(1024, seq)

    w_in = ssm_w_in[0]
    z_end = SSM_D_INNER
    xbc_end = SSM_D_INNER + SSM_CONV_DIM
    wm = jnp.concatenate([w_in[:, z_end:xbc_end], w_in[:, :z_end]], axis=1).astype(BF16)
    wd = jnp.pad(w_in[:, xbc_end:], ((0, 0), (0, LANES - SSM_HEADS))).astype(BF16)
    g0 = norm_pre[0:1]
    head_of_lane = jnp.arange(SSM_D_INNER, dtype=jnp.int32) // SSM_HEAD_DIM
    expand = (jnp.arange(LANES, dtype=jnp.int32)[:, None] == head_of_lane[None, :]).astype(BF16)
    d_lanes = jnp.repeat(ssm_d[0].astype(F32), SSM_HEAD_DIM).reshape(1, SSM_D_INNER)
    ssd_vecs = (ssm_conv_w[0], ssm_conv_b[0:1], _pad_lanes(ssm_dt_bias[0]), _pad_lanes(ssm_a_log[0]),
                d_lanes, expand)
    w_out0 = ssm_w_out[0].astype(BF16)
    nw0 = ssm_norm_w[0:1]
    np0 = norm_post[0:1]

    om_p, od_p = _inproj0(xp, g0, wm, wd, tm_p)
    om_s, od_s = _inproj0(xs, g0, wm, wd, nbs * lq)

    conv0_p = jnp.zeros((nbp, SSM_CONV - 1, SSM_CONV_DIM), F32)
    h0_p = jnp.zeros((nbp, SSM_HEADS, SSM_HEAD_DIM, SSM_STATE), F32)
    y_p, ssm_p, conv_p = _ssd(om_p, od_p, *ssd_vecs, conv0_p, h0_p, nbp, seq // SSM_CHUNK, SSM_CHUNK, SSM_CHUNK)
    y_s, ssm_s, conv_s = _ssd(om_s, od_s, *ssd_vecs, state_conv[0], state_ssm[0], nbs, 1, lq, lq)

    hp1 = _outproj0(y_p, om_p, nw0, w_out0, np0, xp, min(512, seq))
    hs1 = _outproj0(y_s, om_s, nw0, w_out0, np0, xs, nbs * lq)

    wa = attn_w_in[0]
    w = ATTN_WIDTH
    wa_b = wa[:, :4 * w].astype(BF16)
    wa_t = wa.T.astype(BF16)
    wf = jnp.pad(wa[:, 4 * w:], ((0, 0), (0, LANES - ATTN_HEADS))).astype(BF16)
    g1 = norm_pre[1:2]
    np1 = norm_post[1:2]
    w_out1 = attn_w_out[0].astype(BF16)

    qt, kt, vt, vtb, zt, lft = _inproj1p(
        hp1, g1, wa_t[:4 * w], wa_t[4 * w:], attn_b_f[0].astype(F32).reshape(ATTN_HEADS, 1),
        nbp, seq, min(512, seq))
    c_p = _cumsum(lft)
    ka = _kaug(kt, c_p, min(2048, seq))
    tq = min(512, seq)
    ot = _flash(qt, ka, vtb, c_p, tq, tq)
    y_prompt = _outproj1p(ot, zt, w_out1, np1, hp1, min(512, seq))

    proj_s, lf_s = _inproj1s(hs1, g1, wa_b, wf, _pad_lanes(attn_b_f[0]))
    q_s = proj_s[:, :w].reshape(nbs, lq, w)
    k_s = proj_s[:, w:2 * w].reshape(nbs, lq, w)
    v_s = proj_s[:, 2 * w:3 * w].reshape(nbs, lq, w)
    lf_s = lf_s[:, :ATTN_HEADS].reshape(nbs, lq, ATTN_HEADS)
    lfn_t = jnp.pad(lf_s.transpose(0, 2, 1), ((0, 0), (0, 0), (0, LANES - lq)))
    npool = cache_k.shape[1]
    ck = cache_k[0].transpose(0, 2, 3, 1).reshape(npool, w, PAGE_SIZE)
    cv = cache_v[0].transpose(0, 2, 3, 1).reshape(npool, w, PAGE_SIZE)
    clf = cache_logf[0].transpose(0, 2, 1)
    o_s = _decode(q_s, k_s, v_s, lfn_t, ck, cv, clf, page_table, min(DECODE_PAGES_PER_STEP, page_table.shape[1]))
    y_sample = _outproj1(o_s.reshape(nbs * lq, w), proj_s, 3, w_out1, np1, hs1, nbs * lq)

    hd = ATTN_HEAD_DIM
    return (
        y_prompt.reshape(nbp, seq, d),
        y_sample.reshape(nbs, lq, d),
        kt.reshape(nbp, ATTN_HEADS, hd, seq).transpose(0, 3, 1, 2)[None],
        vt.reshape(nbp, ATTN_HEADS, hd, seq).transpose(0, 3, 1, 2)[None],
        lft.transpose(0, 2, 1)[None],
        k_s.reshape(1, nbs, lq, ATTN_HEADS, hd),
        v_s.reshape(1, nbs, lq, ATTN_HEADS, hd),
        lf_s[None],
        ssm_p[None],
        conv_p[None],
        ssm_s[None],
        conv_s[None],
    )
```

```python
import functools

import jax
import jax.numpy as jnp
from jax import lax
from jax.experimental import pallas as pl
from jax.experimental.pallas import tpu as pltpu

F32 = jnp.float32
BF16 = jnp.bfloat16

D_MODEL = 1024
SSM_D_INNER = 2048
SSM_HEAD_DIM = 64
SSM_HEADS = 32
SSM_GROUPS = 8
SSM_HPG = SSM_HEADS // SSM_GROUPS
SSM_STATE = 128
SSM_CONV = 4
SSM_CONV_DIM = SSM_D_INNER + 2 * SSM_GROUPS * SSM_STATE
SSM_CHUNK = 128
ATTN_HEADS = 16
ATTN_HEAD_DIM = 64
ATTN_WIDTH = ATTN_HEADS * ATTN_HEAD_DIM
PAGE_SIZE = 128
RMS_EPS = 1e-6
LANES = 128
SUBLANES = 8
NEG_BIG = -1e30
LOG2E = 1.4426950408889634
AUG = 16
DECODE_PAGES_PER_STEP = 8
DECODE_HEAD_GROUPS = 4
FLASH_HEADS_PER_ITER = 8
VMEM_LIMIT = 56 * 1024 * 1024

_NT = (((1,), (1,)), ((), ()))
_TN = (((0,), (0,)), ((), ()))


def _silu(x):
    return x / (1.0 + jnp.exp(-x))


def _softplus(x):
    return jnp.maximum(x, 0.0) + jnp.log1p(jnp.exp(-jnp.abs(x)))


def _log_sigmoid(x):
    return jnp.minimum(x, 0.0) - jnp.log1p(jnp.exp(-jnp.abs(x)))


def _rms(x):
    return x * lax.rsqrt(jnp.mean(x * x, axis=-1, keepdims=True) + RMS_EPS)


def _split3(x):
    hi = x.astype(BF16)
    r1 = x - hi.astype(F32)
    mid = r1.astype(BF16)
    lo = (r1 - mid.astype(F32)).astype(BF16)
    return hi, mid, lo


def _pieces2(x):
    hi = x.astype(BF16)
    return hi, (x - hi.astype(F32)).astype(BF16)


def _dot3(x, tri):
    n = x.shape[0]
    hi, mid, lo = _split3(x)
    r = jnp.dot(jnp.concatenate([hi, mid, lo], axis=0), tri, preferred_element_type=F32)
    return r[:n] + r[n:2 * n] + r[2 * n:]


def _tri(n, fn):
    r = lax.broadcasted_iota(jnp.int32, (n, n), 0)
    c = lax.broadcasted_iota(jnp.int32, (n, n), 1)
    return jnp.where(fn(r, c), 1.0, 0.0).astype(BF16)


def _cparams(sem, **kw):
    return pltpu.CompilerParams(dimension_semantics=sem, vmem_limit_bytes=VMEM_LIMIT, **kw)


def _inproj0_kernel(x_ref, g_ref, wm_ref, wd_ref, om_ref, od_ref, un_ref):
    @pl.when(pl.program_id(1) == 0)
    def _():
        un = (_rms(x_ref[...]) * g_ref[...]).astype(BF16)
        un_ref[...] = un
        od_ref[...] = jnp.dot(un, wd_ref[...], preferred_element_type=F32)

    om_ref[...] = jnp.dot(un_ref[...], wm_ref[...], preferred_element_type=F32)


def _inproj0(x, g, wm, wd, tm, tn=1024):
    m, d = x.shape
    n = wm.shape[1]
    return pl.pallas_call(
        _inproj0_kernel,
        grid=(m // tm, n // tn),
        in_specs=[
            pl.BlockSpec((tm, d), lambda i, j: (i, 0)),
            pl.BlockSpec((1, d), lambda i, j: (0, 0)),
            pl.BlockSpec((d, tn), lambda i, j: (0, j)),
            pl.BlockSpec((d, LANES), lambda i, j: (0, 0)),
        ],
        out_specs=[
            pl.BlockSpec((tm, tn), lambda i, j: (i, j)),
            pl.BlockSpec((tm, LANES), lambda i, j: (i, 0)),
        ],
        out_shape=[jax.ShapeDtypeStruct((m, n), F32), jax.ShapeDtypeStruct((m, LANES), F32)],
        scratch_shapes=[pltpu.VMEM((tm, d), BF16)],
        compiler_params=_cparams(("parallel", "arbitrary")),
        name="inproj0",
    )(x, g, wm, wd)


def _ssd_kernel(xbc_ref, dt_ref, cw_ref, cb_ref, dtb_ref, alog_ref, dexp_ref, e_ref, conv0_ref, h0_ref,
                y_ref, hout_ref, convout_ref,
                xpad_ref, act_ref, ht_ref, act_t_ref, ex_ref, *, rows_in, n_valid):
    q = SSM_CHUNK
    gw = SSM_HPG * SSM_HEAD_DIM
    c = pl.program_id(1)
    last_c = pl.num_programs(1) - 1
    halo = SUBLANES
    tail0 = halo - (SSM_CONV - 1)

    @pl.when(c == 0)
    def _():
        xpad_ref[0:halo, :] = jnp.zeros((halo, SSM_CONV_DIM), F32)
        xpad_ref[tail0:halo, :] = conv0_ref[0]
        if rows_in < q:
            xpad_ref[halo + rows_in:halo + q, :] = jnp.zeros((q - rows_in, SSM_CONV_DIM), F32)
        for g in range(SSM_GROUPS):
            ht_ref[g] = h0_ref[0, g * SSM_HPG:(g + 1) * SSM_HPG].reshape(gw, SSM_STATE).T

    xpad_ref[halo:halo + rows_in, :] = xbc_ref[...]

    cchunk = 512
    for j0 in range(0, SSM_CONV_DIM, cchunk):
        cs = slice(j0, j0 + cchunk)
        sh = xpad_ref[:, cs]
        acc = cb_ref[:, cs] + sh[halo:] * cw_ref[SSM_CONV - 1:SSM_CONV, cs]
        for k in range(SSM_CONV - 2, -1, -1):
            sh = pltpu.roll(sh, 1, axis=0)
            acc = acc + sh[halo:] * cw_ref[k:k + 1, cs]
        act_ref[:, cs] = _silu(acc)

    new_tail = xpad_ref[halo + n_valid - (SSM_CONV - 1):halo + n_valid, :]
    xpad_ref[tail0:halo, :] = new_tail

    @pl.when(c == last_c)
    def _():
        convout_ref[0] = new_tail

    row = lax.broadcasted_iota(jnp.int32, (q, LANES), 0)
    dt_raw = dt_ref[...]
    if rows_in < q:
        dt_raw = jnp.concatenate([dt_raw, jnp.zeros((q - rows_in, LANES), F32)], axis=0)
    dt = _softplus(dt_raw + dtb_ref[...])
    if n_valid < q:
        dt = jnp.where(row < n_valid, dt, 0.0)
    adt = dt * (-jnp.exp(alog_ref[...]))
    acum = adt
    k = 1
    while k < q:
        acum = acum + jnp.where(row >= k, pltpu.roll(acum, k, axis=0), 0.0)
        k *= 2
    act_t_ref[...] = acum.T
    eac = jnp.exp(acum)
    dec = jnp.exp(acum[q - 1:q, :] - acum)
    for i, val in enumerate((dt, dt * dec, eac)):
        hi, mid = _pieces2(val)
        ex = jnp.dot(jnp.concatenate([hi, mid], axis=0), e_ref[...], preferred_element_type=F32)
        ex_ref[i] = ex[:q] + ex[q:]

    r_i = lax.broadcasted_iota(jnp.int32, (q, q), 0)
    c_i = lax.broadcasted_iota(jnp.int32, (q, q), 1)
    causal = r_i >= c_i
    blk_r = lax.broadcasted_iota(jnp.int32, (SSM_HPG * q, gw), 0) // q
    blk_c = lax.broadcasted_iota(jnp.int32, (SSM_HPG * q, gw), 1) // SSM_HEAD_DIM
    diag_blocks = blk_r == blk_c

    gn = SSM_GROUPS * SSM_STATE
    for g in range(SSM_GROUPS):
        gs = slice(g * gw, (g + 1) * gw)
        bt = act_ref[:, SSM_D_INNER + g * SSM_STATE:SSM_D_INNER + (g + 1) * SSM_STATE].T.astype(BF16)
        c_g = act_ref[:, SSM_D_INNER + gn + g * SSM_STATE:SSM_D_INNER + gn + (g + 1) * SSM_STATE].astype(BF16)
        cb = jnp.dot(c_g, bt, preferred_element_type=F32)
        mats = []
        for r in range(SSM_HPG):
            h = g * SSM_HPG + r
            seg = acum[:, h:h + 1] - act_t_ref[h:h + 1, :]
            mats.append((cb * jnp.exp(jnp.where(causal, seg, -jnp.inf))).astype(BF16))
        mcat = jnp.concatenate(mats, axis=1)
        xs_g = act_ref[:, gs]
        xdt = (xs_g * ex_ref[0, :, gs]).astype(BF16)
        bd = jnp.where(diag_blocks, jnp.concatenate([xdt] * SSM_HPG, axis=0), jnp.zeros((), BF16))
        y_diag = jnp.dot(mcat, bd, preferred_element_type=F32)
        ht = ht_ref[g]
        y_off = jnp.dot(c_g, ht.astype(BF16), preferred_element_type=F32)
        eac_g = ex_ref[2, :, gs]
        y = y_diag + y_off * eac_g + xs_g * dexp_ref[:, gs]
        y_ref[:, gs] = y[:rows_in]
        xdtd = (xs_g * ex_ref[1, :, gs]).astype(BF16)
        ht_ref[g] = ht * eac_g[q - 1:q, :] + jnp.dot(bt, xdtd, preferred_element_type=F32)

    @pl.when(c == last_c)
    def _():
        for g in range(SSM_GROUPS):
            hout_ref[0, g * SSM_HPG:(g + 1) * SSM_HPG] = ht_ref[g].T.reshape(SSM_HPG, SSM_HEAD_DIM, SSM_STATE)


def _ssd(om, od, cw, cb, dtb, alog, dexp, emat, conv0, h0, nb, nc, rows_in, n_valid):
    m = om.shape[0]
    gw = SSM_HPG * SSM_HEAD_DIM
    kern = functools.partial(_ssd_kernel, rows_in=rows_in, n_valid=n_valid)
    vec = lambda: pl.BlockSpec((1, LANES), lambda b, c: (0, 0))
    return pl.pallas_call(
        kern,
        grid=(nb, nc),
        in_specs=[
            pl.BlockSpec((rows_in, SSM_CONV_DIM), lambda b, c: (b * nc + c, 0)),
            pl.BlockSpec((rows_in, LANES), lambda b, c: (b * nc + c, 0)),
            pl.BlockSpec((SSM_CONV, SSM_CONV_DIM), lambda b, c: (0, 0)),
            pl.BlockSpec((1, SSM_CONV_DIM), lambda b, c: (0, 0)),
            vec(), vec(),
            pl.BlockSpec((1, SSM_D_INNER), lambda b, c: (0, 0)),
            pl.BlockSpec((LANES, SSM_D_INNER), lambda b, c: (0, 0)),
            pl.BlockSpec((1, SSM_CONV - 1, SSM_CONV_DIM), lambda b, c: (b, 0, 0)),
            pl.BlockSpec((1, SSM_HEADS, SSM_HEAD_DIM, SSM_STATE), lambda b, c: (b, 0, 0, 0)),
        ],
        out_specs=[
            pl.BlockSpec((rows_in, SSM_D_INNER), lambda b, c: (b * nc + c, 0)),
            pl.BlockSpec((1, SSM_HEADS, SSM_HEAD_DIM, SSM_STATE), lambda b, c: (b, 0, 0, 0)),
            pl.BlockSpec((1, SSM_CONV - 1, SSM_CONV_DIM), lambda b, c: (b, 0, 0)),
        ],
        out_shape=[
            jax.ShapeDtypeStruct((m, SSM_D_INNER), F32),
            jax.ShapeDtypeStruct((nb, SSM_HEADS, SSM_HEAD_DIM, SSM_STATE), F32),
            jax.ShapeDtypeStruct((nb, SSM_CONV - 1, SSM_CONV_DIM), F32),
        ],
        scratch_shapes=[
            pltpu.VMEM((SUBLANES + SSM_CHUNK, SSM_CONV_DIM), F32),
            pltpu.VMEM((SSM_CHUNK, SSM_CONV_DIM), F32),
            pltpu.VMEM((SSM_GROUPS, SSM_STATE, gw), F32),
            pltpu.VMEM((LANES, SSM_CHUNK), F32),
            pltpu.VMEM((3, SSM_CHUNK, SSM_D_INNER), F32),
        ],
        compiler_params=_cparams(("arbitrary", "arbitrary")),
        name="ssd",
    )(om, od, cw, cb, dtb, alog, dexp, emat, conv0, h0)


def _outproj0_kernel(y_ref, z_ref, nw_ref, w_ref, np_ref, x_ref, o_ref):
    h = y_ref[...] * _silu(z_ref[...])
    gw = SSM_D_INNER // SSM_GROUPS
    hn = jnp.concatenate([_rms(h[:, g * gw:(g + 1) * gw]) for g in range(SSM_GROUPS)], axis=1)
    hn = (hn * nw_ref[...]).astype(BF16)
    o = jnp.dot(hn, w_ref[...], preferred_element_type=F32)
    o_ref[...] = x_ref[...] + _rms(o) * np_ref[...]


def _outproj0(y, om, nw, w, npost, x, tm):
    m = y.shape[0]
    zblk = SSM_CONV_DIM // SSM_D_INNER
    return pl.pallas_call(
        _outproj0_kernel,
        grid=(m // tm,),
        in_specs=[
            pl.BlockSpec((tm, SSM_D_INNER), lambda i: (i, 0)),
            pl.BlockSpec((tm, SSM_D_INNER), lambda i: (i, zblk)),
            pl.BlockSpec((1, SSM_D_INNER), lambda i: (0, 0)),
            pl.BlockSpec((SSM_D_INNER, D_MODEL), lambda i: (0, 0)),
            pl.BlockSpec((1, D_MODEL), lambda i: (0, 0)),
            pl.BlockSpec((tm, D_MODEL), lambda i: (i, 0)),
        ],
        out_specs=pl.BlockSpec((tm, D_MODEL), lambda i: (i, 0)),
        out_shape=jax.ShapeDtypeStruct((m, D_MODEL), F32),
        compiler_params=_cparams(("parallel",)),
        name="outproj0",
    )(y, om, nw, w, npost, x)


def _inproj1p_kernel(x_ref, g_ref, wt_ref, wft_ref, bf_ref, qt_ref, kt_ref, vt_ref, vtb_ref, zt_ref, lft_ref):
    un = (_rms(x_ref[...]) * g_ref[...]).astype(BF16)
    w = ATTN_WIDTH
    proj = lambda i: lax.dot_general(wt_ref[i * w:(i + 1) * w, :], un, _NT, preferred_element_type=F32)
    qt_ref[0] = (proj(0) * (ATTN_HEAD_DIM ** -0.5 * LOG2E)).astype(BF16)
    kt_ref[0] = proj(1)
    vt = proj(2)
    vt_ref[0] = vt
    vtb_ref[0] = vt.astype(BF16)
    zt_ref[0] = proj(3)
    ft = lax.dot_general(wft_ref[...], un, _NT, preferred_element_type=F32)
    lft_ref[0] = _log_sigmoid(ft + bf_ref[...])


def _inproj1p(x, g, wt, wft, bfc, nb, seq, tm):
    m, d = x.shape
    nq = seq // tm
    w = ATTN_WIDTH
    full = lambda r, c: pl.BlockSpec((r, c), lambda b, i: (0, 0))
    feat = lambda r: pl.BlockSpec((1, r, tm), lambda b, i: (b, 0, i))
    fshape = lambda dt: jax.ShapeDtypeStruct((nb, w, seq), dt)
    return pl.pallas_call(
        _inproj1p_kernel,
        grid=(nb, nq),
        in_specs=[pl.BlockSpec((tm, d), lambda b, i: (b * nq + i, 0)), full(1, d), full(4 * w, d),
                  full(ATTN_HEADS, d), full(ATTN_HEADS, 1)],
        out_specs=[feat(w), feat(w), feat(w), feat(w), feat(w), feat(ATTN_HEADS)],
        out_shape=[fshape(BF16), fshape(F32), fshape(F32), fshape(BF16), fshape(F32),
                   jax.ShapeDtypeStruct((nb, ATTN_HEADS, seq), F32)],
        compiler_params=_cparams(("parallel", "parallel")),
        name="inproj1p",
    )(x, g, wt, wft, bfc)


def _inproj1s_kernel(x_ref, g_ref, w_ref, wf_ref, bf_ref, o_ref, lf_ref):
    un = (_rms(x_ref[...]) * g_ref[...]).astype(BF16)
    o_ref[...] = jnp.dot(un, w_ref[...], preferred_element_type=F32)
    lf_ref[...] = _log_sigmoid(jnp.dot(un, wf_ref[...], preferred_element_type=F32) + bf_ref[...])


def _inproj1s(x, g, w, wf, bfr):
    m, d = x.shape
    n = w.shape[1]
    return pl.pallas_call(
        _inproj1s_kernel,
        grid=(1,),
        in_specs=[pl.BlockSpec((m, d), lambda i: (0, 0)), pl.BlockSpec((1, d), lambda i: (0, 0)),
                  pl.BlockSpec((d, n), lambda i: (0, 0)), pl.BlockSpec((d, LANES), lambda i: (0, 0)),
                  pl.BlockSpec((1, LANES), lambda i: (0, 0))],
        out_specs=[pl.BlockSpec((m, n), lambda i: (0, 0)), pl.BlockSpec((m, LANES), lambda i: (0, 0))],
        out_shape=[jax.ShapeDtypeStruct((m, n), F32), jax.ShapeDtypeStruct((m, LANES), F32)],
        compiler_params=_cparams(("arbitrary",)),
        name="inproj1s",
    )(x, g, w, wf, bfr)


def _cumsum_kernel(x_ref, o_ref):
    nblk = x_ref.shape[2] // LANES
    upper = _tri(LANES, lambda r, c: r <= c)

    def body(j, carry):
        off = pl.multiple_of(j * LANES, LANES)
        loc = _dot3(x_ref[0, :, pl.ds(off, LANES)], upper)
        o_ref[0, :, pl.ds(off, LANES)] = loc + carry
        return carry + loc[:, LANES - 1:LANES]

    lax.fori_loop(0, nblk, body, jnp.zeros((x_ref.shape[1], 1), F32))


def _cumsum(x):
    nb, h, seq = x.shape
    return pl.pallas_call(
        _cumsum_kernel,
        grid=(nb,),
        in_specs=[pl.BlockSpec((1, h, seq), lambda b: (b, 0, 0))],
        out_specs=pl.BlockSpec((1, h, seq), lambda b: (b, 0, 0)),
        out_shape=jax.ShapeDtypeStruct((nb, h, seq), F32),
        compiler_params=_cparams(("parallel",)),
        name="lf_cumsum",
    )(x)


def _bias_rows(c2, first, n, width):
    hi, mid, lo = (p.astype(F32) for p in _split3(c2))
    row = lax.broadcasted_iota(jnp.int32, (n, width), 0)
    ones_first = 3 - first
    out = jnp.where((row >= ones_first) & (row < ones_first + 3), 1.0, 0.0)
    for i, piece in enumerate((hi, mid, lo)):
        out = jnp.where(row == first + i, piece, out)
    return out


def _kaug_kernel(kt_ref, c_ref, o_ref, *, chunk):
    h = pl.program_id(1)

    def body(j, carry):
        off = pl.multiple_of(j * chunk, chunk)
        c2 = c_ref[0, pl.ds(h, 1), pl.ds(off, chunk)] * LOG2E
        full = jnp.concatenate([kt_ref[0, :, pl.ds(off, chunk)],
                                _bias_rows(-c2, 3, ATTN_HEAD_DIM, chunk)], axis=0)
        o_ref[0, 0, pl.ds(off, chunk), :] = full.T.astype(BF16)
        return carry

    lax.fori_loop(0, kt_ref.shape[2] // chunk, body, 0)


def _kaug(kt, c, chunk):
    nb, w, seq = kt.shape
    return pl.pallas_call(
        functools.partial(_kaug_kernel, chunk=chunk),
        grid=(nb, ATTN_HEADS),
        in_specs=[pl.BlockSpec((1, ATTN_HEAD_DIM, seq), lambda b, h: (b, h, 0)),
                  pl.BlockSpec((1, ATTN_HEADS, seq), lambda b, h: (b, 0, 0))],
        out_specs=pl.BlockSpec((1, 1, seq, LANES), lambda b, h: (b, h, 0, 0)),
        out_shape=jax.ShapeDtypeStruct((nb, ATTN_HEADS, seq, LANES), BF16),
        compiler_params=_cparams(("parallel", "parallel")),
        name="fox_kaug",
    )(kt, c)


def _flash_kernel(qt_ref, ka_ref, vt_ref, cq_ref, o_ref, qa_ref, m_ref, acc_ref, s_ref, *, tq, tk):
    qi = pl.program_id(1)
    kj = pl.program_id(2)
    hd = ATTN_HEAD_DIM

    @pl.when(kj == 0)
    def _():
        m_ref[...] = jnp.full(m_ref.shape, NEG_BIG, F32)
        acc_ref[...] = jnp.zeros(acc_ref.shape, F32)

        def build(h, carry):
            off = pl.multiple_of(h * hd, hd)
            c2 = cq_ref[0, pl.ds(h, 1), :] * LOG2E
            qa_ref[h, 0:hd, :] = qt_ref[0, pl.ds(off, hd), :]
            qa_ref[h, hd:2 * hd, :] = _bias_rows(c2, 0, hd, tq).astype(BF16)
            return carry

        lax.fori_loop(0, ATTN_HEADS, build, 0)

    def step(masked):
        if masked:
            key = kj * tk + lax.broadcasted_iota(jnp.int32, (tk, tq), 0)
            qry = qi * tq + lax.broadcasted_iota(jnp.int32, (tk, tq), 1)
            keep = key <= qry
        ones = jnp.ones((AUG, tk), BF16)

        def scores(h, slot):
            st = jnp.dot(ka_ref[0, h], qa_ref[h], preferred_element_type=F32)
            if masked:
                st = jnp.where(keep, st, NEG_BIG)
            s_ref[slot] = st
            return jnp.max(st, axis=0, keepdims=True)

        def softmax_pv(h, slot, mx):
            off = pl.multiple_of(h * hd, hd)
            m_prev = m_ref[h]
            m_new = jnp.maximum(m_prev, mx)
            pt = jnp.exp2(s_ref[slot] - m_new[0:1]).astype(BF16)
            va = jnp.concatenate([vt_ref[0, pl.ds(off, hd), :], ones], axis=0)
            pv = jnp.dot(va, pt, preferred_element_type=F32)
            acc_ref[h] = acc_ref[h] * jnp.exp2(m_prev - m_new)[0:1] + pv
            m_ref[h] = m_new

        def run(h0, n, mx):
            for k in range(n):
                nxt = scores(h0 + k + 1, (k + 1) % 2)
                softmax_pv(h0 + k, k % 2, mx)
                mx = nxt
            return mx

        per_iter = FLASH_HEADS_PER_ITER
        n_iter = ATTN_HEADS // per_iter - 1
        mx = lax.fori_loop(0, n_iter, lambda i, mx: run(i * per_iter, per_iter, mx), scores(0, 0))
        mx = run(n_iter * per_iter, per_iter - 1, mx)
        softmax_pv(ATTN_HEADS - 1, (per_iter - 1) % 2, mx)

    @pl.when(kj < qi)
    def _():
        step(False)

    @pl.when(kj == qi)
    def _():
        step(True)

        def fin(h, carry):
            off = pl.multiple_of(h * hd, hd)
            a = acc_ref[h]
            o_ref[0, pl.ds(off, hd), :] = a[0:hd] / a[hd:hd + 1]
            return carry

        lax.fori_loop(0, ATTN_HEADS, fin, 0)


def _flash(qt, ka, vtb, c, tq, tk):
    nb, w, seq = qt.shape
    nq, nk = seq // tq, seq // tk
    kern = functools.partial(_flash_kernel, tq=tq, tk=tk)
    return pl.pallas_call(
        kern,
        grid=(nb, nq, nk),
        in_specs=[
            pl.BlockSpec((1, w, tq), lambda b, i, j: (b, 0, i)),
            pl.BlockSpec((1, ATTN_HEADS, tk, LANES), lambda b, i, j: (b, 0, jnp.minimum(j, i), 0)),
            pl.BlockSpec((1, w, tk), lambda b, i, j: (b, 0, jnp.minimum(j, i))),
            pl.BlockSpec((1, ATTN_HEADS, tq), lambda b, i, j: (b, 0, i)),
        ],
        out_specs=pl.BlockSpec((1, w, tq), lambda b, i, j: (b, 0, i)),
        out_shape=jax.ShapeDtypeStruct((nb, w, seq), F32),
        scratch_shapes=[
            pltpu.VMEM((ATTN_HEADS, 2 * ATTN_HEAD_DIM, tq), BF16),
            pltpu.VMEM((ATTN_HEADS, SUBLANES, tq), F32),
            pltpu.VMEM((ATTN_HEADS, ATTN_HEAD_DIM + AUG, tq), F32),
            pltpu.VMEM((2, tk, tq), F32),
        ],
        compiler_params=_cparams(("parallel", "parallel", "arbitrary")),
        name="fox_flash",
    )(qt, ka, vtb, c)


def _decode_kernel(pt_ref, q_ref, kn_ref, vn_ref, lfn_ref, *refs, pps, lq):
    del pt_ref
    k_refs = refs[0:pps]
    v_refs = refs[pps:2 * pps]
    lf_refs = refs[2 * pps:3 * pps]
    o_ref, qg_ref, m_ref, l_ref, acc_ref, carry_ref = refs[3 * pps:]
    j = pl.program_id(1)
    nrow = ATTN_HEADS * lq
    hd = ATTN_HEAD_DIM
    ngrp = DECODE_HEAD_GROUPS
    hpg = ATTN_HEADS // ngrp
    grow = hpg * lq
    gcol = hpg * hd
    rsl = lambda g: slice(g * grow, (g + 1) * grow)
    csl = lambda g: slice(g * gcol, (g + 1) * gcol)

    @pl.when(j == 0)
    def _():
        rh = lax.broadcasted_iota(jnp.int32, (grow, gcol), 0) // lq
        ch = lax.broadcasted_iota(jnp.int32, (grow, gcol), 1) // hd
        for g in range(ngrp):
            qt = jnp.concatenate([q_ref[0, :, csl(g)] * (hd ** -0.5)] * hpg, axis=0)
            qg_ref[g] = jnp.where(rh == ch, qt, 0.0).astype(BF16)
        m_ref[...] = jnp.full(m_ref.shape, NEG_BIG, F32)
        l_ref[...] = jnp.zeros(l_ref.shape, F32)
        acc_ref[...] = jnp.zeros(acc_ref.shape, F32)
        carry_ref[...] = jnp.zeros(carry_ref.shape, F32)

    def expand_heads(r):
        return jnp.concatenate(
            [jnp.broadcast_to(r[h:h + 1, :], (lq, r.shape[1])) for h in range(ATTN_HEADS)], axis=0)

    def merge(s_list, pv_fn):
        m_prev = m_ref[...]
        m_new = m_prev
        for s in s_list:
            m_new = jnp.maximum(m_new, jnp.max(s, axis=1, keepdims=True))
        alpha = jnp.exp(m_prev - m_new)
        lsum = jnp.zeros_like(m_prev)
        pv = [None] * ngrp
        for i, s in enumerate(s_list):
            p = jnp.exp(s - m_new)
            lsum = lsum + jnp.sum(p, axis=1, keepdims=True)
            pb = p.astype(BF16)
            for g in range(ngrp):
                d = pv_fn(i, g, pb[rsl(g)])
                pv[g] = d if pv[g] is None else pv[g] + d
        l_ref[...] = alpha * l_ref[...] + lsum
        for g in range(ngrp):
            acc_ref[g] = alpha[rsl(g)] * acc_ref[g] + pv[g]
        m_ref[...] = m_new

    later = _tri(PAGE_SIZE, lambda r, c: r > c)
    carry = carry_ref[...]
    s_list, v_list = [], []
    for p in range(pps):
        lf = lf_refs[p][0]
        r = _dot3(lf, later) + carry
        carry = carry + jnp.sum(lf, axis=1, keepdims=True)
        kt = k_refs[p][0].astype(BF16)
        s = jnp.concatenate([jnp.dot(qg_ref[g], kt[csl(g)], preferred_element_type=F32)
                             for g in range(ngrp)], axis=0)
        s_list.append(s + expand_heads(r))
        v_list.append(v_refs[p][0].astype(BF16))
    carry_ref[...] = carry
    merge(s_list, lambda i, g, pr: lax.dot_general(pr, v_list[i][csl(g)], _NT, preferred_element_type=F32))

    @pl.when(j == pl.num_programs(1) - 1)
    def _():
        pad = jnp.zeros((LANES - lq, ATTN_WIDTH), F32)
        kn = jnp.concatenate([kn_ref[0], pad], axis=0).astype(BF16)
        vn = jnp.concatenate([vn_ref[0], pad], axis=0).astype(BF16)
        s = jnp.concatenate([lax.dot_general(qg_ref[g], kn[:, csl(g)], _NT, preferred_element_type=F32)
                             for g in range(ngrp)], axis=0)
        cn = _dot3(lfn_ref[0], _tri(LANES, lambda r, c: r <= c))
        s = s - expand_heads(cn)
        qpos = lax.broadcasted_iota(jnp.int32, (nrow, LANES), 0) % lq
        kpos = lax.broadcasted_iota(jnp.int32, (nrow, LANES), 1)
        s = jnp.where(kpos <= qpos, s, NEG_BIG)
        merge([s], lambda i, g, pr: jnp.dot(pr, vn[:, csl(g)], preferred_element_type=F32))
        for g in range(ngrp):
            out = acc_ref[g] / l_ref[rsl(g), :]
            for hl in range(hpg):
                h = g * hpg + hl
                o_ref[0, :, h * hd:(h + 1) * hd] = out[hl * lq:(hl + 1) * lq, hl * hd:(hl + 1) * hd]


def _decode(q, kn, vn, lfn_t, ck, cv, clf, page_table, pps):
    nb, lq, w = q.shape
    npages = page_table.shape[1]
    nsteps = npages // pps
    nrow = ATTN_HEADS * lq
    kern = functools.partial(_decode_kernel, pps=pps, lq=lq)

    def page_spec(shape, p):
        return pl.BlockSpec(shape, lambda b, j, pt: (pt[b, npages - 1 - (j * pps + p)], 0, 0))

    tokspec = lambda: pl.BlockSpec((1, lq, w), lambda b, j, pt: (b, 0, 0))
    in_specs = [tokspec(), tokspec(), tokspec(),
                pl.BlockSpec((1, ATTN_HEADS, LANES), lambda b, j, pt: (b, 0, 0))]
    in_specs += [page_spec((1, w, PAGE_SIZE), p) for p in range(pps)]
    in_specs += [page_spec((1, w, PAGE_SIZE), p) for p in range(pps)]
    in_specs += [page_spec((1, ATTN_HEADS, PAGE_SIZE), p) for p in range(pps)]
    return pl.pallas_call(
        kern,
        grid_spec=pltpu.PrefetchScalarGridSpec(
            num_scalar_prefetch=1,
            grid=(nb, nsteps),
            in_specs=in_specs,
            out_specs=pl.BlockSpec((1, lq, w), lambda b, j, pt: (b, 0, 0)),
            scratch_shapes=[
                pltpu.VMEM((DECODE_HEAD_GROUPS, nrow // DECODE_HEAD_GROUPS, w // DECODE_HEAD_GROUPS), BF16),
                pltpu.VMEM((nrow, 1), F32),
                pltpu.VMEM((nrow, 1), F32),
                pltpu.VMEM((DECODE_HEAD_GROUPS, nrow // DECODE_HEAD_GROUPS, w // DECODE_HEAD_GROUPS), F32),
                pltpu.VMEM((ATTN_HEADS, 1), F32),
            ],
        ),
        out_shape=jax.ShapeDtypeStruct((nb, lq, w), F32),
        compiler_params=_cparams(("parallel", "arbitrary")),
        name="fox_decode",
    )(page_table, q, kn, vn, lfn_t, *([ck] * pps), *([cv] * pps), *([clf] * pps))


def _outproj1_kernel(o_ref, z_ref, w_ref, np_ref, x_ref, out_ref):
    g = (o_ref[...] * _silu(z_ref[...])).astype(BF16)
    y = jnp.dot(g, w_ref[...], preferred_element_type=F32)
    out_ref[...] = x_ref[...] + _rms(y) * np_ref[...]


def _outproj1(o, z, zblk, w, npost, x, tm):
    m = o.shape[0]
    return pl.pallas_call(
        _outproj1_kernel,
        grid=(m // tm,),
        in_specs=[
            pl.BlockSpec((tm, ATTN_WIDTH), lambda i: (i, 0)),
            pl.BlockSpec((tm, ATTN_WIDTH), lambda i: (i, zblk)),
            pl.BlockSpec((ATTN_WIDTH, D_MODEL), lambda i: (0, 0)),
            pl.BlockSpec((1, D_MODEL), lambda i: (0, 0)),
            pl.BlockSpec((tm, D_MODEL), lambda i: (i, 0)),
        ],
        out_specs=pl.BlockSpec((tm, D_MODEL), lambda i: (i, 0)),
        out_shape=jax.ShapeDtypeStruct((m, D_MODEL), F32),
        compiler_params=_cparams(("parallel",)),
        name="outproj1",
    )(o, z, w, npost, x)


def _outproj1p_kernel(ot_ref, zt_ref, w_ref, np_ref, x_ref, out_ref):
    gt = (ot_ref[0] * _silu(zt_ref[0])).astype(BF16)
    y = lax.dot_general(gt, w_ref[...], _TN, preferred_element_type=F32)
    out_ref[...] = x_ref[...] + _rms(y) * np_ref[...]


def _outproj1p(ot, zt, w, npost, x, tm):
    nb, wd, seq = ot.shape
    nq = seq // tm
    feat = lambda: pl.BlockSpec((1, wd, tm), lambda b, i: (b, 0, i))
    return pl.pallas_call(
        _outproj1p_kernel,
        grid=(nb, nq),
        in_specs=[feat(), feat(),
                  pl.BlockSpec((wd, D_MODEL), lambda b, i: (0, 0)),
                  pl.BlockSpec((1, D_MODEL), lambda b, i: (0, 0)),
                  pl.BlockSpec((tm, D_MODEL), lambda b, i: (b * nq + i, 0))],
        out_specs=pl.BlockSpec((tm, D_MODEL), lambda b, i: (b * nq + i, 0)),
        out_shape=jax.ShapeDtypeStruct((nb * seq, D_MODEL), F32),
        compiler_params=_cparams(("parallel", "parallel")),
        name="outproj1p",
    )(ot, zt, w, npost, x)


def _pad_lanes(v):
    return jnp.pad(v.astype(F32), (0, LANES - v.shape[0])).reshape(1, LANES)


def kernel(x_prompt, x_sample, state_ssm, state_conv, cache_k, cache_v, cache_logf, page_table, norm_pre, norm_post, ssm_w_in, ssm_conv_w, ssm_conv_b, ssm_dt_bias, ssm_a_log, ssm_d, ssm_norm_w, ssm_w_out, attn_w_in, attn_b_f, attn_w_out):
    nbp, seq, d = x_prompt.shape
    nbs, lq, _ = x_sample.shape
    xp = x_prompt.reshape(nbp * seq, d)
    xs = x_sample.reshape(nbs * lq, d)
    tm_p = min(1024, seq)

    w_in = ssm_w_in[0]
    z_end = SSM_D_INNER
    xbc_end = SSM_D_INNER + SSM_CONV_DIM
    wm = jnp.concatenate([w_in[:, z_end:xbc_end], w_in[:, :z_end]], axis=1).astype(BF16)
    wd = jnp.pad(w_in[:, xbc_end:], ((0, 0), (0, LANES - SSM_HEADS))).astype(BF16)
    g0 = norm_pre[0:1]
    head_of_lane = jnp.arange(SSM_D_INNER, dtype=jnp.int32) // SSM_HEAD_DIM
    expand = (jnp.arange(LANES, dtype=jnp.int32)[:, None] == head_of_lane[None, :]).astype(BF16)
    d_lanes = jnp.repeat(ssm_d[0].astype(F32), SSM_HEAD_DIM).reshape(1, SSM_D_INNER)
    ssd_vecs = (ssm_conv_w[0], ssm_conv_b[0:1], _pad_lanes(ssm_dt_bias[0]), _pad_lanes(ssm_a_log[0]),
                d_lanes, expand)
    w_out0 = ssm_w_out[0].astype(BF16)
    nw0 = ssm_norm_w[0:1]
    np0 = norm_post[0:1]

    om_p, od_p = _inproj0(xp, g0, wm, wd, tm_p)
    om_s, od_s = _inproj0(xs, g0, wm, wd, nbs * lq)

    conv0_p = jnp.zeros((nbp, SSM_CONV - 1, SSM_CONV_DIM), F32)
    h0_p = jnp.zeros((nbp, SSM_HEADS, SSM_HEAD_DIM, SSM_STATE), F32)
    y_p, ssm_p, conv_p = _ssd(om_p, od_p, *ssd_vecs, conv0_p, h0_p, nbp, seq // SSM_CHUNK, SSM_CHUNK, SSM_CHUNK)
    y_s, ssm_s, conv_s = _ssd(om_s, od_s, *ssd_vecs, state_conv[0], state_ssm[0], nbs, 1, lq, lq)

    hp1 = _outproj0(y_p, om_p, nw0, w_out0, np0, xp, min(512, seq))
    hs1 = _outproj0(y_s, om_s, nw0, w_out0, np0, xs, nbs * lq)

    wa = attn_w_in[0]
    w = ATTN_WIDTH
    wa_b = wa[:, :4 * w].astype(BF16)
    wa_t = wa.T.astype(BF16)
    wf = jnp.pad(wa[:, 4 * w:], ((0, 0), (0, LANES - ATTN_HEADS))).astype(BF16)
    g1 = norm_pre[1:2]
    np1 = norm_post[1:2]
    w_out1 = attn_w_out[0].astype(BF16)

    qt, kt, vt, vtb, zt, lft = _inproj1p(
        hp1, g1, wa_t[:4 * w], wa_t[4 * w:], attn_b_f[0].astype(F32).reshape(ATTN_HEADS, 1),
        nbp, seq, min(512, seq))
    c_p = _cumsum(lft)
    ka = _kaug(kt, c_p, min(1024, seq))
    tq = min(512, seq)
    ot = _flash(qt, ka, vtb, c_p, tq, tq)
    y_prompt = _outproj1p(ot, zt, w_out1, np1, hp1, min(512, seq))

    proj_s, lf_s = _inproj1s(hs1, g1, wa_b, wf, _pad_lanes(attn_b_f[0]))
    q_s = proj_s[:, :w].reshape(nbs, lq, w)
    k_s = proj_s[:, w:2 * w].reshape(nbs, lq, w)
    v_s = proj_s[:, 2 * w:3 * w].reshape(nbs, lq, w)
    lf_s = lf_s[:, :ATTN_HEADS].reshape(nbs, lq, ATTN_HEADS)
    lfn_t = jnp.pad(lf_s.transpose(0, 2, 1), ((0, 0), (0, 0), (0, LANES - lq)))
    npool = cache_k.shape[1]
    ck = cache_k[0].transpose(0, 2, 3, 1).reshape(npool, w, PAGE_SIZE)
    cv = cache_v[0].transpose(0, 2, 3, 1).reshape(npool, w, PAGE_SIZE)
    clf = cache_logf[0].transpose(0, 2, 1)
    o_s = _decode(q_s, k_s, v_s, lfn_t, ck, cv, clf, page_table, min(DECODE_PAGES_PER_STEP, page_table.shape[1]))
    y_sample = _outproj1(o_s.reshape(nbs * lq, w), proj_s, 3, w_out1, np1, hs1, nbs * lq)

    hd = ATTN_HEAD_DIM
    return (
        y_prompt.reshape(nbp, seq, d),
        y_sample.reshape(nbs, lq, d),
        kt.reshape(nbp, ATTN_HEADS, hd, seq).transpose(0, 3, 1, 2)[None],
        vt.reshape(nbp, ATTN_HEADS, hd, seq).transpose(0, 3, 1, 2)[None],
        lft.transpose(0, 2, 1)[None],
        k_s.reshape(1, nbs, lq, ATTN_HEADS, hd),
        v_s.reshape(1, nbs, lq, ATTN_HEADS, hd),
        lf_s[None],
        ssm_p[None],
        conv_p[None],
        ssm_s[None],
        conv_s[None],
    )
```

```python
import functools

import jax
import jax.numpy as jnp
from jax import lax
from jax.experimental import pallas as pl
from jax.experimental.pallas import tpu as pltpu

F32 = jnp.float32
BF16 = jnp.bfloat16

D_MODEL = 1024
SSM_D_INNER = 2048
SSM_HEAD_DIM = 64
SSM_HEADS = 32
SSM_GROUPS = 8
SSM_HPG = SSM_HEADS // SSM_GROUPS
SSM_STATE = 128
SSM_CONV = 4
SSM_CONV_DIM = SSM_D_INNER + 2 * SSM_GROUPS * SSM_STATE
SSM_CHUNK = 128
ATTN_HEADS = 16
ATTN_HEAD_DIM = 64
ATTN_WIDTH = ATTN_HEADS * ATTN_HEAD_DIM
PAGE_SIZE = 128
RMS_EPS = 1e-6
LANES = 128
SUBLANES = 8
NEG_BIG = -1e30
LOG2E = 1.4426950408889634
AUG = 16
DECODE_PAGES_PER_STEP = 16
DECODE_PAGES_PER_MERGE = 8
ATTN_VMEM_LIMIT = 60 * 1024 * 1024
DECODE_HEAD_GROUPS = 4
FLASH_HEADS_PER_ITER = 8
VMEM_LIMIT = 56 * 1024 * 1024

_NT = (((1,), (1,)), ((), ()))
_TN = (((0,), (0,)), ((), ()))


def _silu(x):
    return x / (1.0 + jnp.exp(-x))


def _softplus(x):
    return jnp.maximum(x, 0.0) + jnp.log1p(jnp.exp(-jnp.abs(x)))


def _log_sigmoid(x):
    return jnp.minimum(x, 0.0) - jnp.log1p(jnp.exp(-jnp.abs(x)))


def _rms(x):
    return x * lax.rsqrt(jnp.mean(x * x, axis=-1, keepdims=True) + RMS_EPS)


def _split3(x):
    hi = x.astype(BF16)
    r1 = x - hi.astype(F32)
    mid = r1.astype(BF16)
    lo = (r1 - mid.astype(F32)).astype(BF16)
    return hi, mid, lo


def _pieces2(x):
    hi = x.astype(BF16)
    return hi, (x - hi.astype(F32)).astype(BF16)


def _dot3(x, tri):
    n = x.shape[0]
    hi, mid, lo = _split3(x)
    r = jnp.dot(jnp.concatenate([hi, mid, lo], axis=0), tri, preferred_element_type=F32)
    return r[:n] + r[n:2 * n] + r[2 * n:]


def _tri(n, fn):
    r = lax.broadcasted_iota(jnp.int32, (n, n), 0)
    c = lax.broadcasted_iota(jnp.int32, (n, n), 1)
    return jnp.where(fn(r, c), 1.0, 0.0).astype(BF16)


def _cparams(sem, **kw):
    return pltpu.CompilerParams(dimension_semantics=sem, vmem_limit_bytes=VMEM_LIMIT, **kw)


def _inproj0_kernel(x_ref, g_ref, wm_ref, wd_ref, om_ref, od_ref, un_ref):
    @pl.when(pl.program_id(1) == 0)
    def _():
        un = (_rms(x_ref[...]) * g_ref[...]).astype(BF16)
        un_ref[...] = un
        od_ref[...] = jnp.dot(un, wd_ref[...], preferred_element_type=F32)

    om_ref[...] = jnp.dot(un_ref[...], wm_ref[...], preferred_element_type=F32)


def _inproj0(x, g, wm, wd, tm, tn=1024):
    m, d = x.shape
    n = wm.shape[1]
    return pl.pallas_call(
        _inproj0_kernel,
        grid=(m // tm, n // tn),
        in_specs=[
            pl.BlockSpec((tm, d), lambda i, j: (i, 0)),
            pl.BlockSpec((1, d), lambda i, j: (0, 0)),
            pl.BlockSpec((d, tn), lambda i, j: (0, j)),
            pl.BlockSpec((d, LANES), lambda i, j: (0, 0)),
        ],
        out_specs=[
            pl.BlockSpec((tm, tn), lambda i, j: (i, j)),
            pl.BlockSpec((tm, LANES), lambda i, j: (i, 0)),
        ],
        out_shape=[jax.ShapeDtypeStruct((m, n), F32), jax.ShapeDtypeStruct((m, LANES), F32)],
        scratch_shapes=[pltpu.VMEM((tm, d), BF16)],
        compiler_params=_cparams(("parallel", "arbitrary")),
        name="inproj0",
    )(x, g, wm, wd)


def _ssd_kernel(xbc_ref, dt_ref, cw_ref, cb_ref, dtb_ref, alog_ref, dexp_ref, e_ref, conv0_ref, h0_ref,
                y_ref, hout_ref, convout_ref,
                xpad_ref, act_ref, ht_ref, act_t_ref, ex_ref, *, rows_in, n_valid):
    q = SSM_CHUNK
    gw = SSM_HPG * SSM_HEAD_DIM
    c = pl.program_id(1)
    last_c = pl.num_programs(1) - 1
    halo = SUBLANES
    tail0 = halo - (SSM_CONV - 1)

    @pl.when(c == 0)
    def _():
        xpad_ref[0:halo, :] = jnp.zeros((halo, SSM_CONV_DIM), F32)
        xpad_ref[tail0:halo, :] = conv0_ref[0]
        if rows_in < q:
            xpad_ref[halo + rows_in:halo + q, :] = jnp.zeros((q - rows_in, SSM_CONV_DIM), F32)
        for g in range(SSM_GROUPS):
            ht_ref[g] = h0_ref[0, g * SSM_HPG:(g + 1) * SSM_HPG].reshape(gw, SSM_STATE).T

    xpad_ref[halo:halo + rows_in, :] = xbc_ref[...]

    cchunk = 512
    for j0 in range(0, SSM_CONV_DIM, cchunk):
        cs = slice(j0, j0 + cchunk)
        sh = xpad_ref[:, cs]
        acc = cb_ref[:, cs] + sh[halo:] * cw_ref[SSM_CONV - 1:SSM_CONV, cs]
        for k in range(SSM_CONV - 2, -1, -1):
            sh = pltpu.roll(sh, 1, axis=0)
            acc = acc + sh[halo:] * cw_ref[k:k + 1, cs]
        act_ref[:, cs] = _silu(acc)

    new_tail = xpad_ref[halo + n_valid - (SSM_CONV - 1):halo + n_valid, :]
    xpad_ref[tail0:halo, :] = new_tail

    @pl.when(c == last_c)
    def _():
        convout_ref[0] = new_tail

    row = lax.broadcasted_iota(jnp.int32, (q, LANES), 0)
    dt_raw = dt_ref[...]
    if rows_in < q:
        dt_raw = jnp.concatenate([dt_raw, jnp.zeros((q - rows_in, LANES), F32)], axis=0)
    dt = _softplus(dt_raw + dtb_ref[...])
    if n_valid < q:
        dt = jnp.where(row < n_valid, dt, 0.0)
    adt = dt * (-jnp.exp(alog_ref[...]))
    acum = adt
    k = 1
    while k < q:
        acum = acum + jnp.where(row >= k, pltpu.roll(acum, k, axis=0), 0.0)
        k *= 2
    act_t_ref[...] = acum.T
    eac = jnp.exp(acum)
    dec = jnp.exp(acum[q - 1:q, :] - acum)
    for i, val in enumerate((dt, dt * dec, eac)):
        hi, mid = _pieces2(val)
        ex = jnp.dot(jnp.concatenate([hi, mid], axis=0), e_ref[...], preferred_element_type=F32)
        ex_ref[i] = ex[:q] + ex[q:]

    r_i = lax.broadcasted_iota(jnp.int32, (q, q), 0)
    c_i = lax.broadcasted_iota(jnp.int32, (q, q), 1)
    causal = r_i >= c_i
    blk_r = lax.broadcasted_iota(jnp.int32, (SSM_HPG * q, gw), 0) // q
    blk_c = lax.broadcasted_iota(jnp.int32, (SSM_HPG * q, gw), 1) // SSM_HEAD_DIM
    diag_blocks = blk_r == blk_c

    gn = SSM_GROUPS * SSM_STATE
    for g in range(SSM_GROUPS):
        gs = slice(g * gw, (g + 1) * gw)
        bt = act_ref[:, SSM_D_INNER + g * SSM_STATE:SSM_D_INNER + (g + 1) * SSM_STATE].T.astype(BF16)
        c_g = act_ref[:, SSM_D_INNER + gn + g * SSM_STATE:SSM_D_INNER + gn + (g + 1) * SSM_STATE].astype(BF16)
        cb = jnp.dot(c_g, bt, preferred_element_type=F32)
        mats = []
        for r in range(SSM_HPG):
            h = g * SSM_HPG + r
            seg = acum[:, h:h + 1] - act_t_ref[h:h + 1, :]
            mats.append((cb * jnp.exp(jnp.where(causal, seg, -jnp.inf))).astype(BF16))
        mcat = jnp.concatenate(mats, axis=1)
        xs_g = act_ref[:, gs]
        xdt = (xs_g * ex_ref[0, :, gs]).astype(BF16)
        bd = jnp.where(diag_blocks, jnp.concatenate([xdt] * SSM_HPG, axis=0), jnp.zeros((), BF16))
        y_diag = jnp.dot(mcat, bd, preferred_element_type=F32)
        ht = ht_ref[g]
        y_off = jnp.dot(c_g, ht.astype(BF16), preferred_element_type=F32)
        eac_g = ex_ref[2, :, gs]
        y = y_diag + y_off * eac_g + xs_g * dexp_ref[:, gs]
        y_ref[:, gs] = y[:rows_in]
        xdtd = (xs_g * ex_ref[1, :, gs]).astype(BF16)
        ht_ref[g] = ht * eac_g[q - 1:q, :] + jnp.dot(bt, xdtd, preferred_element_type=F32)

    @pl.when(c == last_c)
    def _():
        for g in range(SSM_GROUPS):
            hout_ref[0, g * SSM_HPG:(g + 1) * SSM_HPG] = ht_ref[g].T.reshape(SSM_HPG, SSM_HEAD_DIM, SSM_STATE)


def _ssd(om, od, cw, cb, dtb, alog, dexp, emat, conv0, h0, nb, nc, rows_in, n_valid):
    m = om.shape[0]
    gw = SSM_HPG * SSM_HEAD_DIM
    kern = functools.partial(_ssd_kernel, rows_in=rows_in, n_valid=n_valid)
    vec = lambda: pl.BlockSpec((1, LANES), lambda b, c: (0, 0))
    return pl.pallas_call(
        kern,
        grid=(nb, nc),
        in_specs=[
            pl.BlockSpec((rows_in, SSM_CONV_DIM), lambda b, c: (b * nc + c, 0)),
            pl.BlockSpec((rows_in, LANES), lambda b, c: (b * nc + c, 0)),
            pl.BlockSpec((SSM_CONV, SSM_CONV_DIM), lambda b, c: (0, 0)),
            pl.BlockSpec((1, SSM_CONV_DIM), lambda b, c: (0, 0)),
            vec(), vec(),
            pl.BlockSpec((1, SSM_D_INNER), lambda b, c: (0, 0)),
            pl.BlockSpec((LANES, SSM_D_INNER), lambda b, c: (0, 0)),
            pl.BlockSpec((1, SSM_CONV - 1, SSM_CONV_DIM), lambda b, c: (b, 0, 0)),
            pl.BlockSpec((1, SSM_HEADS, SSM_HEAD_DIM, SSM_STATE), lambda b, c: (b, 0, 0, 0)),
        ],
        out_specs=[
            pl.BlockSpec((rows_in, SSM_D_INNER), lambda b, c: (b * nc + c, 0)),
            pl.BlockSpec((1, SSM_HEADS, SSM_HEAD_DIM, SSM_STATE), lambda b, c: (b, 0, 0, 0)),
            pl.BlockSpec((1, SSM_CONV - 1, SSM_CONV_DIM), lambda b, c: (b, 0, 0)),
        ],
        out_shape=[
            jax.ShapeDtypeStruct((m, SSM_D_INNER), F32),
            jax.ShapeDtypeStruct((nb, SSM_HEADS, SSM_HEAD_DIM, SSM_STATE), F32),
            jax.ShapeDtypeStruct((nb, SSM_CONV - 1, SSM_CONV_DIM), F32),
        ],
        scratch_shapes=[
            pltpu.VMEM((SUBLANES + SSM_CHUNK, SSM_CONV_DIM), F32),
            pltpu.VMEM((SSM_CHUNK, SSM_CONV_DIM), F32),
            pltpu.VMEM((SSM_GROUPS, SSM_STATE, gw), F32),
            pltpu.VMEM((LANES, SSM_CHUNK), F32),
            pltpu.VMEM((3, SSM_CHUNK, SSM_D_INNER), F32),
        ],
        compiler_params=_cparams(("arbitrary", "arbitrary")),
        name="ssd",
    )(om, od, cw, cb, dtb, alog, dexp, emat, conv0, h0)


def _outproj0_kernel(y_ref, z_ref, nw_ref, w_ref, np_ref, x_ref, o_ref):
    h = y_ref[...] * _silu(z_ref[...])
    gw = SSM_D_INNER // SSM_GROUPS
    hn = jnp.concatenate([_rms(h[:, g * gw:(g + 1) * gw]) for g in range(SSM_GROUPS)], axis=1)
    hn = (hn * nw_ref[...]).astype(BF16)
    o = jnp.dot(hn, w_ref[...], preferred_element_type=F32)
    o_ref[...] = x_ref[...] + _rms(o) * np_ref[...]


def _outproj0(y, om, nw, w, npost, x, tm):
    m = y.shape[0]
    zblk = SSM_CONV_DIM // SSM_D_INNER
    return pl.pallas_call(
        _outproj0_kernel,
        grid=(m // tm,),
        in_specs=[
            pl.BlockSpec((tm, SSM_D_INNER), lambda i: (i, 0)),
            pl.BlockSpec((tm, SSM_D_INNER), lambda i: (i, zblk)),
            pl.BlockSpec((1, SSM_D_INNER), lambda i: (0, 0)),
            pl.BlockSpec((SSM_D_INNER, D_MODEL), lambda i: (0, 0)),
            pl.BlockSpec((1, D_MODEL), lambda i: (0, 0)),
            pl.BlockSpec((tm, D_MODEL), lambda i: (i, 0)),
        ],
        out_specs=pl.BlockSpec((tm, D_MODEL), lambda i: (i, 0)),
        out_shape=jax.ShapeDtypeStruct((m, D_MODEL), F32),
        compiler_params=_cparams(("parallel",)),
        name="outproj0",
    )(y, om, nw, w, npost, x)


def _inproj1p_kernel(x_ref, g_ref, wt_ref, wft_ref, bf_ref, qt_ref, kt_ref, vt_ref, vtb_ref, zt_ref, lft_ref):
    un = (_rms(x_ref[...]) * g_ref[...]).astype(BF16)
    w = ATTN_WIDTH
    proj = lambda i: lax.dot_general(wt_ref[i * w:(i + 1) * w, :], un, _NT, preferred_element_type=F32)
    qt_ref[0] = (proj(0) * (ATTN_HEAD_DIM ** -0.5 * LOG2E)).astype(BF16)
    kt_ref[0] = proj(1)
    vt = proj(2)
    vt_ref[0] = vt
    vtb_ref[0] = vt.astype(BF16)
    zt_ref[0] = proj(3)
    ft = lax.dot_general(wft_ref[...], un, _NT, preferred_element_type=F32)
    lft_ref[0] = _log_sigmoid(ft + bf_ref[...])


def _inproj1p(x, g, wt, wft, bfc, nb, seq, tm):
    m, d = x.shape
    nq = seq // tm
    w = ATTN_WIDTH
    full = lambda r, c: pl.BlockSpec((r, c), lambda b, i: (0, 0))
    feat = lambda r: pl.BlockSpec((1, r, tm), lambda b, i: (b, 0, i))
    fshape = lambda dt: jax.ShapeDtypeStruct((nb, w, seq), dt)
    return pl.pallas_call(
        _inproj1p_kernel,
        grid=(nb, nq),
        in_specs=[pl.BlockSpec((tm, d), lambda b, i: (b * nq + i, 0)), full(1, d), full(4 * w, d),
                  full(ATTN_HEADS, d), full(ATTN_HEADS, 1)],
        out_specs=[feat(w), feat(w), feat(w), feat(w), feat(w), feat(ATTN_HEADS)],
        out_shape=[fshape(BF16), fshape(F32), fshape(F32), fshape(BF16), fshape(F32),
                   jax.ShapeDtypeStruct((nb, ATTN_HEADS, seq), F32)],
        compiler_params=_cparams(("parallel", "parallel")),
        name="inproj1p",
    )(x, g, wt, wft, bfc)


def _inproj1s_kernel(x_ref, g_ref, w_ref, wf_ref, bf_ref, o_ref, lf_ref):
    un = (_rms(x_ref[...]) * g_ref[...]).astype(BF16)
    o_ref[...] = jnp.dot(un, w_ref[...], preferred_element_type=F32)
    lf_ref[...] = _log_sigmoid(jnp.dot(un, wf_ref[...], preferred_element_type=F32) + bf_ref[...])


def _inproj1s(x, g, w, wf, bfr):
    m, d = x.shape
    n = w.shape[1]
    return pl.pallas_call(
        _inproj1s_kernel,
        grid=(1,),
        in_specs=[pl.BlockSpec((m, d), lambda i: (0, 0)), pl.BlockSpec((1, d), lambda i: (0, 0)),
                  pl.BlockSpec((d, n), lambda i: (0, 0)), pl.BlockSpec((d, LANES), lambda i: (0, 0)),
                  pl.BlockSpec((1, LANES), lambda i: (0, 0))],
        out_specs=[pl.BlockSpec((m, n), lambda i: (0, 0)), pl.BlockSpec((m, LANES), lambda i: (0, 0))],
        out_shape=[jax.ShapeDtypeStruct((m, n), F32), jax.ShapeDtypeStruct((m, LANES), F32)],
        compiler_params=_cparams(("arbitrary",)),
        name="inproj1s",
    )(x, g, w, wf, bfr)


def _cumsum_kernel(x_ref, o_ref):
    nblk = x_ref.shape[2] // LANES
    upper = _tri(LANES, lambda r, c: r <= c)

    def body(j, carry):
        off = pl.multiple_of(j * LANES, LANES)
        loc = _dot3(x_ref[0, :, pl.ds(off, LANES)], upper)
        o_ref[0, :, pl.ds(off, LANES)] = loc + carry
        return carry + loc[:, LANES - 1:LANES]

    lax.fori_loop(0, nblk, body, jnp.zeros((x_ref.shape[1], 1), F32))


def _cumsum(x):
    nb, h, seq = x.shape
    return pl.pallas_call(
        _cumsum_kernel,
        grid=(nb,),
        in_specs=[pl.BlockSpec((1, h, seq), lambda b: (b, 0, 0))],
        out_specs=pl.BlockSpec((1, h, seq), lambda b: (b, 0, 0)),
        out_shape=jax.ShapeDtypeStruct((nb, h, seq), F32),
        compiler_params=_cparams(("parallel",)),
        name="lf_cumsum",
    )(x)


def _bias_rows(c2, first, n, width):
    hi, mid, lo = (p.astype(F32) for p in _split3(c2))
    row = lax.broadcasted_iota(jnp.int32, (n, width), 0)
    ones_first = 3 - first
    out = jnp.where((row >= ones_first) & (row < ones_first + 3), 1.0, 0.0)
    for i, piece in enumerate((hi, mid, lo)):
        out = jnp.where(row == first + i, piece, out)
    return out


def _kaug_kernel(kt_ref, c_ref, o_ref, *, chunk):
    h = pl.program_id(1)

    def body(j, carry):
        off = pl.multiple_of(j * chunk, chunk)
        c2 = c_ref[0, pl.ds(h, 1), pl.ds(off, chunk)] * LOG2E
        full = jnp.concatenate([kt_ref[0, :, pl.ds(off, chunk)],
                                _bias_rows(-c2, 3, ATTN_HEAD_DIM, chunk)], axis=0)
        o_ref[0, 0, pl.ds(off, chunk), :] = full.T.astype(BF16)
        return carry

    lax.fori_loop(0, kt_ref.shape[2] // chunk, body, 0)


def _kaug(kt, c, chunk):
    nb, w, seq = kt.shape
    return pl.pallas_call(
        functools.partial(_kaug_kernel, chunk=chunk),
        grid=(nb, ATTN_HEADS),
        in_specs=[pl.BlockSpec((1, ATTN_HEAD_DIM, seq), lambda b, h: (b, h, 0)),
                  pl.BlockSpec((1, ATTN_HEADS, seq), lambda b, h: (b, 0, 0))],
        out_specs=pl.BlockSpec((1, 1, seq, LANES), lambda b, h: (b, h, 0, 0)),
        out_shape=jax.ShapeDtypeStruct((nb, ATTN_HEADS, seq, LANES), BF16),
        compiler_params=_cparams(("parallel", "parallel")),
        name="fox_kaug",
    )(kt, c)


def _flash_body(qi, kj, qt_ref, ka_ref, vt_ref, cq_ref, o_ref, qa_ref, m_ref, acc_ref, s_ref, *, tq, tk):
    hd = ATTN_HEAD_DIM

    @pl.when(kj == 0)
    def _():
        m_ref[...] = jnp.full(m_ref.shape, NEG_BIG, F32)
        acc_ref[...] = jnp.zeros(acc_ref.shape, F32)

        def build(h, carry):
            off = pl.multiple_of(h * hd, hd)
            c2 = cq_ref[0, pl.ds(h, 1), :] * LOG2E
            qa_ref[h, 0:hd, :] = qt_ref[0, pl.ds(off, hd), :]
            qa_ref[h, hd:2 * hd, :] = _bias_rows(c2, 0, hd, tq).astype(BF16)
            return carry

        lax.fori_loop(0, ATTN_HEADS, build, 0)

    def step(masked):
        if masked:
            key = kj * tk + lax.broadcasted_iota(jnp.int32, (tk, tq), 0)
            qry = qi * tq + lax.broadcasted_iota(jnp.int32, (tk, tq), 1)
            keep = key <= qry
        ones = jnp.ones((AUG, tk), BF16)

        def scores(h, slot):
            st = jnp.dot(ka_ref[0, h], qa_ref[h], preferred_element_type=F32)
            if masked:
                st = jnp.where(keep, st, NEG_BIG)
            s_ref[slot] = st
            return jnp.max(st, axis=0, keepdims=True)

        def softmax_pv(h, slot, mx):
            off = pl.multiple_of(h * hd, hd)
            m_prev = m_ref[h]
            m_new = jnp.maximum(m_prev, mx)
            pt = jnp.exp2(s_ref[slot] - m_new[0:1]).astype(BF16)
            va = jnp.concatenate([vt_ref[0, pl.ds(off, hd), :], ones], axis=0)
            pv = jnp.dot(va, pt, preferred_element_type=F32)
            acc_ref[h] = acc_ref[h] * jnp.exp2(m_prev - m_new)[0:1] + pv
            m_ref[h] = m_new

        def run(h0, n, mx):
            for k in range(n):
                nxt = scores(h0 + k + 1, (k + 1) % 2)
                softmax_pv(h0 + k, k % 2, mx)
                mx = nxt
            return mx

        per_iter = FLASH_HEADS_PER_ITER
        n_iter = ATTN_HEADS // per_iter - 1
        mx = lax.fori_loop(0, n_iter, lambda i, mx: run(i * per_iter, per_iter, mx), scores(0, 0))
        mx = run(n_iter * per_iter, per_iter - 1, mx)
        softmax_pv(ATTN_HEADS - 1, (per_iter - 1) % 2, mx)

    @pl.when(kj < qi)
    def _():
        step(False)

    @pl.when(kj == qi)
    def _():
        step(True)

        def fin(h, carry):
            off = pl.multiple_of(h * hd, hd)
            a = acc_ref[h]
            o_ref[0, pl.ds(off, hd), :] = a[0:hd] / a[hd:hd + 1]
            return carry

        lax.fori_loop(0, ATTN_HEADS, fin, 0)


def _decode_body(j, last_j, q_ref, kn_ref, vn_ref, lfn_ref, k_refs, v_refs, lf_refs,
                 o_ref, qg_ref, m_ref, l_ref, acc_ref, carry_ref, *, lq):
    pps = len(k_refs)
    nrow = ATTN_HEADS * lq
    hd = ATTN_HEAD_DIM
    ngrp = DECODE_HEAD_GROUPS
    hpg = ATTN_HEADS // ngrp
    grow = hpg * lq
    gcol = hpg * hd
    rsl = lambda g: slice(g * grow, (g + 1) * grow)
    csl = lambda g: slice(g * gcol, (g + 1) * gcol)

    @pl.when(j == 0)
    def _():
        rh = lax.broadcasted_iota(jnp.int32, (grow, gcol), 0) // lq
        ch = lax.broadcasted_iota(jnp.int32, (grow, gcol), 1) // hd
        for g in range(ngrp):
            qt = jnp.concatenate([q_ref[0, :, csl(g)] * (hd ** -0.5)] * hpg, axis=0)
            qg_ref[g] = jnp.where(rh == ch, qt, 0.0).astype(BF16)
        m_ref[...] = jnp.full(m_ref.shape, NEG_BIG, F32)
        l_ref[...] = jnp.zeros(l_ref.shape, F32)
        acc_ref[...] = jnp.zeros(acc_ref.shape, F32)
        carry_ref[...] = jnp.zeros(carry_ref.shape, F32)

    def expand_heads(r):
        return jnp.concatenate(
            [jnp.broadcast_to(r[h:h + 1, :], (lq, r.shape[1])) for h in range(ATTN_HEADS)], axis=0)

    def merge(s_list, pv_fn):
        m_prev = m_ref[...]
        m_new = m_prev
        for s in s_list:
            m_new = jnp.maximum(m_new, jnp.max(s, axis=1, keepdims=True))
        alpha = jnp.exp(m_prev - m_new)
        lsum = jnp.zeros_like(m_prev)
        pv = [None] * ngrp
        for i, s in enumerate(s_list):
            p = jnp.exp(s - m_new)
            lsum = lsum + jnp.sum(p, axis=1, keepdims=True)
            pb = p.astype(BF16)
            for g in range(ngrp):
                d = pv_fn(i, g, pb[rsl(g)])
                pv[g] = d if pv[g] is None else pv[g] + d
        l_ref[...] = alpha * l_ref[...] + lsum
        for g in range(ngrp):
            acc_ref[g] = alpha[rsl(g)] * acc_ref[g] + pv[g]
        m_ref[...] = m_new

    later = _tri(PAGE_SIZE, lambda r, c: r > c)
    carry = carry_ref[...]
    for p0 in range(0, pps, DECODE_PAGES_PER_MERGE):
        s_list, v_list = [], []
        for p in range(p0, min(p0 + DECODE_PAGES_PER_MERGE, pps)):
            lf = lf_refs[p][0]
            r = _dot3(lf, later) + carry
            carry = carry + jnp.sum(lf, axis=1, keepdims=True)
            kt = k_refs[p][0].astype(BF16)
            s = jnp.concatenate([jnp.dot(qg_ref[g], kt[csl(g)], preferred_element_type=F32)
                                 for g in range(ngrp)], axis=0)
            s_list.append(s + expand_heads(r))
            v_list.append(v_refs[p][0].astype(BF16))
        merge(s_list, lambda i, g, pr, v_list=v_list: lax.dot_general(
            pr, v_list[i][csl(g)], _NT, preferred_element_type=F32))
    carry_ref[...] = carry

    @pl.when(j == last_j)
    def _():
        pad = jnp.zeros((LANES - lq, ATTN_WIDTH), F32)
        kn = jnp.concatenate([kn_ref[0], pad], axis=0).astype(BF16)
        vn = jnp.concatenate([vn_ref[0], pad], axis=0).astype(BF16)
        s = jnp.concatenate([lax.dot_general(qg_ref[g], kn[:, csl(g)], _NT, preferred_element_type=F32)
                             for g in range(ngrp)], axis=0)
        cn = _dot3(lfn_ref[0], _tri(LANES, lambda r, c: r <= c))
        s = s - expand_heads(cn)
        qpos = lax.broadcasted_iota(jnp.int32, (nrow, LANES), 0) % lq
        kpos = lax.broadcasted_iota(jnp.int32, (nrow, LANES), 1)
        s = jnp.where(kpos <= qpos, s, NEG_BIG)
        merge([s], lambda i, g, pr: jnp.dot(pr, vn[:, csl(g)], preferred_element_type=F32))
        for g in range(ngrp):
            out = acc_ref[g] / l_ref[rsl(g), :]
            for hl in range(hpg):
                h = g * hpg + hl
                o_ref[0, :, h * hd:(h + 1) * hd] = out[hl * lq:(hl + 1) * lq, hl * hd:(hl + 1) * hd]


def _attn_kernel(bt_ref, qit_ref, kjt_ref, pt_ref, qt_ref, ka_ref, vt_ref, cq_ref,
                 qs_ref, kn_ref, vn_ref, lfn_ref, *refs, tq, tk, pps, lq, n_dec, spe):
    del bt_ref, pt_ref
    k_refs = refs[0:pps]
    v_refs = refs[pps:2 * pps]
    lf_refs = refs[2 * pps:3 * pps]
    (o_ref, os_ref, qa_ref, m_ref, acc_ref, s_ref, qg_ref, md_ref, ld_ref, accd_ref, carry_ref) = refs[3 * pps:]
    t = pl.program_id(0)
    _flash_body(qit_ref[t], kjt_ref[t], qt_ref, ka_ref, vt_ref, cq_ref, o_ref, qa_ref, m_ref, acc_ref, s_ref,
                tq=tq, tk=tk)

    @pl.when(t < n_dec)
    def _():
        _decode_body(t % spe, spe - 1, qs_ref, kn_ref, vn_ref, lfn_ref, k_refs, v_refs, lf_refs,
                     os_ref, qg_ref, md_ref, ld_ref, accd_ref, carry_ref, lq=lq)


def _attention(qt, ka, vtb, c, q_s, kn, vn, lfn_t, ck, cv, clf, page_table, tq, pps):
    nb, w, seq = qt.shape
    nbs, lq, _ = q_s.shape
    npages = page_table.shape[1]
    nq = seq // tq
    tri = [(b, i, j) for b in range(nb) for i in range(nq) for j in range(i + 1)]
    nsteps = len(tri)
    spe = npages // pps
    n_dec = nbs * spe
    assert n_dec <= nsteps and npages % pps == 0, (n_dec, nsteps)
    bt, qit, kjt = (jnp.asarray([x[k] for x in tri], jnp.int32) for k in range(3))
    nrow = ATTN_HEADS * lq
    ng = DECODE_HEAD_GROUPS
    kern = functools.partial(_attn_kernel, tq=tq, tk=tq, pps=pps, lq=lq, n_dec=n_dec, spe=spe)

    def dec_elem(t):
        return jnp.minimum(t // spe, nbs - 1)

    def page_spec(shape, p):
        def imap(t, bt, qit, kjt, pt):
            td = jnp.minimum(t, n_dec - 1)
            return (pt[td // spe, npages - 1 - ((td % spe) * pps + p)], 0, 0)
        return pl.BlockSpec(shape, imap)

    tokspec = lambda: pl.BlockSpec((1, lq, w), lambda t, bt, qit, kjt, pt: (dec_elem(t), 0, 0))
    in_specs = [
        pl.BlockSpec((1, w, tq), lambda t, bt, qit, kjt, pt: (bt[t], 0, qit[t])),
        pl.BlockSpec((1, ATTN_HEADS, tq, LANES), lambda t, bt, qit, kjt, pt: (bt[t], 0, kjt[t], 0)),
        pl.BlockSpec((1, w, tq), lambda t, bt, qit, kjt, pt: (bt[t], 0, kjt[t])),
        pl.BlockSpec((1, ATTN_HEADS, tq), lambda t, bt, qit, kjt, pt: (bt[t], 0, qit[t])),
        tokspec(), tokspec(), tokspec(),
        pl.BlockSpec((1, ATTN_HEADS, LANES), lambda t, bt, qit, kjt, pt: (dec_elem(t), 0, 0)),
    ]
    in_specs += [page_spec((1, w, PAGE_SIZE), p) for p in range(pps)]
    in_specs += [page_spec((1, w, PAGE_SIZE), p) for p in range(pps)]
    in_specs += [page_spec((1, ATTN_HEADS, PAGE_SIZE), p) for p in range(pps)]
    return pl.pallas_call(
        kern,
        grid_spec=pltpu.PrefetchScalarGridSpec(
            num_scalar_prefetch=4,
            grid=(nsteps,),
            in_specs=in_specs,
            out_specs=[
                pl.BlockSpec((1, w, tq), lambda t, bt, qit, kjt, pt: (bt[t], 0, qit[t])),
                pl.BlockSpec((1, lq, w), lambda t, bt, qit, kjt, pt: (dec_elem(t), 0, 0)),
            ],
            scratch_shapes=[
                pltpu.VMEM((ATTN_HEADS, 2 * ATTN_HEAD_DIM, tq), BF16),
                pltpu.VMEM((ATTN_HEADS, SUBLANES, tq), F32),
                pltpu.VMEM((ATTN_HEADS, ATTN_HEAD_DIM + AUG, tq), F32),
                pltpu.VMEM((2, tq, tq), F32),
                pltpu.VMEM((ng, nrow // ng, w // ng), BF16),
                pltpu.VMEM((nrow, 1), F32),
                pltpu.VMEM((nrow, 1), F32),
                pltpu.VMEM((ng, nrow // ng, w // ng), F32),
                pltpu.VMEM((ATTN_HEADS, 1), F32),
            ],
        ),
        out_shape=[jax.ShapeDtypeStruct((nb, w, seq), F32), jax.ShapeDtypeStruct((nbs, lq, w), F32)],
        compiler_params=pltpu.CompilerParams(dimension_semantics=("arbitrary",), vmem_limit_bytes=ATTN_VMEM_LIMIT),
        name="fox_attention",
    )(bt, qit, kjt, page_table, qt, ka, vtb, c, q_s, kn, vn, lfn_t, *([ck] * pps), *([cv] * pps), *([clf] * pps))


def _outproj1_kernel(o_ref, z_ref, w_ref, np_ref, x_ref, out_ref):
    g = (o_ref[...] * _silu(z_ref[...])).astype(BF16)
    y = jnp.dot(g, w_ref[...], preferred_element_type=F32)
    out_ref[...] = x_ref[...] + _rms(y) * np_ref[...]


def _outproj1(o, z, zblk, w, npost, x, tm):
    m = o.shape[0]
    return pl.pallas_call(
        _outproj1_kernel,
        grid=(m // tm,),
        in_specs=[
            pl.BlockSpec((tm, ATTN_WIDTH), lambda i: (i, 0)),
            pl.BlockSpec((tm, ATTN_WIDTH), lambda i: (i, zblk)),
            pl.BlockSpec((ATTN_WIDTH, D_MODEL), lambda i: (0, 0)),
            pl.BlockSpec((1, D_MODEL), lambda i: (0, 0)),
            pl.BlockSpec((tm, D_MODEL), lambda i: (i, 0)),
        ],
        out_specs=pl.BlockSpec((tm, D_MODEL), lambda i: (i, 0)),
        out_shape=jax.ShapeDtypeStruct((m, D_MODEL), F32),
        compiler_params=_cparams(("parallel",)),
        name="outproj1",
    )(o, z, w, npost, x)


def _outproj1p_kernel(ot_ref, zt_ref, w_ref, np_ref, x_ref, out_ref):
    gt = (ot_ref[0] * _silu(zt_ref[0])).astype(BF16)
    y = lax.dot_general(gt, w_ref[...], _TN, preferred_element_type=F32)
    out_ref[...] = x_ref[...] + _rms(y) * np_ref[...]


def _outproj1p(ot, zt, w, npost, x, tm):
    nb, wd, seq = ot.shape
    nq = seq // tm
    feat = lambda: pl.BlockSpec((1, wd, tm), lambda b, i: (b, 0, i))
    return pl.pallas_call(
        _outproj1p_kernel,
        grid=(nb, nq),
        in_specs=[feat(), feat(),
                  pl.BlockSpec((wd, D_MODEL), lambda b, i: (0, 0)),
                  pl.BlockSpec((1, D_MODEL), lambda b, i: (0, 0)),
                  pl.BlockSpec((tm, D_MODEL), lambda b, i: (b * nq + i, 0))],
        out_specs=pl.BlockSpec((tm, D_MODEL), lambda b, i: (b * nq + i, 0)),
        out_shape=jax.ShapeDtypeStruct((nb * seq, D_MODEL), F32),
        compiler_params=_cparams(("parallel", "parallel")),
        name="outproj1p",
    )(ot, zt, w, npost, x)


def _pad_lanes(v):
    return jnp.pad(v.astype(F32), (0, LANES - v.shape[0])).reshape(1, LANES)


def kernel(x_prompt, x_sample, state_ssm, state_conv, cache_k, cache_v, cache_logf, page_table, norm_pre, norm_post, ssm_w_in, ssm_conv_w, ssm_conv_b, ssm_dt_bias, ssm_a_log, ssm_d, ssm_norm_w, ssm_w_out, attn_w_in, attn_b_f, attn_w_out):
    nbp, seq, d = x_prompt.shape
    nbs, lq, _ = x_sample.shape
    xp = x_prompt.reshape(nbp * seq, d)
    xs = x_sample.reshape(nbs * lq, d)
    tm_p = min(1024, seq)

    w_in = ssm_w_in[0]
    z_end = SSM_D_INNER
    xbc_end = SSM_D_INNER + SSM_CONV_DIM
    wm = jnp.concatenate([w_in[:, z_end:xbc_end], w_in[:, :z_end]], axis=1).astype(BF16)
    wd = jnp.pad(w_in[:, xbc_end:], ((0, 0), (0, LANES - SSM_HEADS))).astype(BF16)
    g0 = norm_pre[0:1]
    head_of_lane = jnp.arange(SSM_D_INNER, dtype=jnp.int32) // SSM_HEAD_DIM
    expand = (jnp.arange(LANES, dtype=jnp.int32)[:, None] == head_of_lane[None, :]).astype(BF16)
    d_lanes = jnp.repeat(ssm_d[0].astype(F32), SSM_HEAD_DIM).reshape(1, SSM_D_INNER)
    ssd_vecs = (ssm_conv_w[0], ssm_conv_b[0:1], _pad_lanes(ssm_dt_bias[0]), _pad_lanes(ssm_a_log[0]),
                d_lanes, expand)
    w_out0 = ssm_w_out[0].astype(BF16)
    nw0 = ssm_norm_w[0:1]
    np0 = norm_post[0:1]

    om_p, od_p = _inproj0(xp, g0, wm, wd, tm_p)
    om_s, od_s = _inproj0(xs, g0, wm, wd, nbs * lq)

    conv0_p = jnp.zeros((nbp, SSM_CONV - 1, SSM_CONV_DIM), F32)
    h0_p = jnp.zeros((nbp, SSM_HEADS, SSM_HEAD_DIM, SSM_STATE), F32)
    y_p, ssm_p, conv_p = _ssd(om_p, od_p, *ssd_vecs, conv0_p, h0_p, nbp, seq // SSM_CHUNK, SSM_CHUNK, SSM_CHUNK)
    y_s, ssm_s, conv_s = _ssd(om_s, od_s, *ssd_vecs, state_conv[0], state_ssm[0], nbs, 1, lq, lq)

    hp1 = _outproj0(y_p, om_p, nw0, w_out0, np0, xp, min(512, seq))
    hs1 = _outproj0(y_s, om_s, nw0, w_out0, np0, xs, nbs * lq)

    wa = attn_w_in[0]
    w = ATTN_WIDTH
    wa_b = wa[:, :4 * w].astype(BF16)
    wa_t = wa.T.astype(BF16)
    wf = jnp.pad(wa[:, 4 * w:], ((0, 0), (0, LANES - ATTN_HEADS))).astype(BF16)
    g1 = norm_pre[1:2]
    np1 = norm_post[1:2]
    w_out1 = attn_w_out[0].astype(BF16)

    qt, kt, vt, vtb, zt, lft = _inproj1p(
        hp1, g1, wa_t[:4 * w], wa_t[4 * w:], attn_b_f[0].astype(F32).reshape(ATTN_HEADS, 1),
        nbp, seq, min(512, seq))
    c_p = _cumsum(lft)
    ka = _kaug(kt, c_p, min(1024, seq))
    tq = min(512, seq)

    proj_s, lf_s = _inproj1s(hs1, g1, wa_b, wf, _pad_lanes(attn_b_f[0]))
    q_s = proj_s[:, :w].reshape(nbs, lq, w)
    k_s = proj_s[:, w:2 * w].reshape(nbs, lq, w)
    v_s = proj_s[:, 2 * w:3 * w].reshape(nbs, lq, w)
    lf_s = lf_s[:, :ATTN_HEADS].reshape(nbs, lq, ATTN_HEADS)
    lfn_t = jnp.pad(lf_s.transpose(0, 2, 1), ((0, 0), (0, 0), (0, LANES - lq)))
    npool = cache_k.shape[1]
    ck = cache_k[0].transpose(0, 2, 3, 1).reshape(npool, w, PAGE_SIZE)
    cv = cache_v[0].transpose(0, 2, 3, 1).reshape(npool, w, PAGE_SIZE)
    clf = cache_logf[0].transpose(0, 2, 1)
    ot, o_s = _attention(qt, ka, vtb, c_p, q_s, k_s, v_s, lfn_t, ck, cv, clf, page_table, tq,
                         min(DECODE_PAGES_PER_STEP, page_table.shape[1]))
    y_prompt = _outproj1p(ot, zt, w_out1, np1, hp1, min(512, seq))
    y_sample = _outproj1(o_s.reshape(nbs * lq, w), proj_s, 3, w_out1, np1, hs1, nbs * lq)

    hd = ATTN_HEAD_DIM
    return (
        y_prompt.reshape(nbp, seq, d),
        y_sample.reshape(nbs, lq, d),
        kt.reshape(nbp, ATTN_HEADS, hd, seq).transpose(0, 3, 1, 2)[None],
        vt.reshape(nbp, ATTN_HEADS, hd, seq).transpose(0, 3, 1, 2)[None],
        lft.transpose(0, 2, 1)[None],
        k_s.reshape(1, nbs, lq, ATTN_HEADS, hd),
        v_s.reshape(1, nbs, lq, ATTN_HEADS, hd),
        lf_s[None],
        ssm_p[None],
        conv_p[None],
        ssm_s[None],
        conv_s[None],
    )
```

```python
import functools

import jax
import jax.numpy as jnp
from jax import lax
from jax.experimental import pallas as pl
from jax.experimental.pallas import tpu as pltpu

F32 = jnp.float32
BF16 = jnp.bfloat16

D_MODEL = 1024
SSM_D_INNER = 2048
SSM_HEAD_DIM = 64
SSM_HEADS = 32
SSM_GROUPS = 8
SSM_HPG = SSM_HEADS // SSM_GROUPS
SSM_STATE = 128
SSM_CONV = 4
SSM_CONV_DIM = SSM_D_INNER + 2 * SSM_GROUPS * SSM_STATE
SSM_CHUNK = 128
ATTN_HEADS = 16
ATTN_HEAD_DIM = 64
ATTN_WIDTH = ATTN_HEADS * ATTN_HEAD_DIM
PAGE_SIZE = 128
RMS_EPS = 1e-6
LANES = 128
SUBLANES = 8
NEG_BIG = -1e30
LOG2E = 1.4426950408889634
AUG = 16
DECODE_PAGES_PER_STEP = 16
DECODE_PAGES_PER_MERGE = 8
ATTN_VMEM_LIMIT = 60 * 1024 * 1024
DECODE_HEAD_GROUPS = 4
FLASH_HEADS_PER_ITER = 8
VMEM_LIMIT = 56 * 1024 * 1024

_NT = (((1,), (1,)), ((), ()))
_TN = (((0,), (0,)), ((), ()))


def _silu(x):
    return x / (1.0 + jnp.exp(-x))


def _softplus(x):
    return jnp.maximum(x, 0.0) + jnp.log1p(jnp.exp(-jnp.abs(x)))


def _log_sigmoid(x):
    return jnp.minimum(x, 0.0) - jnp.log1p(jnp.exp(-jnp.abs(x)))


def _rms(x):
    return x * lax.rsqrt(jnp.mean(x * x, axis=-1, keepdims=True) + RMS_EPS)


def _split3(x):
    hi = x.astype(BF16)
    r1 = x - hi.astype(F32)
    mid = r1.astype(BF16)
    lo = (r1 - mid.astype(F32)).astype(BF16)
    return hi, mid, lo


def _pieces2(x):
    hi = x.astype(BF16)
    return hi, (x - hi.astype(F32)).astype(BF16)


def _dot3(x, tri):
    n = x.shape[0]
    hi, mid, lo = _split3(x)
    r = jnp.dot(jnp.concatenate([hi, mid, lo], axis=0), tri, preferred_element_type=F32)
    return r[:n] + r[n:2 * n] + r[2 * n:]


def _tri(n, fn):
    r = lax.broadcasted_iota(jnp.int32, (n, n), 0)
    c = lax.broadcasted_iota(jnp.int32, (n, n), 1)
    return jnp.where(fn(r, c), 1.0, 0.0).astype(BF16)


def _cparams(sem, **kw):
    return pltpu.CompilerParams(dimension_semantics=sem, vmem_limit_bytes=VMEM_LIMIT, **kw)


def _inproj0_kernel(x_ref, g_ref, wm_ref, wd_ref, om_ref, od_ref, un_ref):
    @pl.when(pl.program_id(1) == 0)
    def _():
        un = (_rms(x_ref[...]) * g_ref[...]).astype(BF16)
        un_ref[...] = un
        od_ref[...] = jnp.dot(un, wd_ref[...], preferred_element_type=F32)

    om_ref[...] = jnp.dot(un_ref[...], wm_ref[...], preferred_element_type=F32)


def _inproj0(x, g, wm, wd, tm, tn=1024):
    m, d = x.shape
    n = wm.shape[1]
    return pl.pallas_call(
        _inproj0_kernel,
        grid=(m // tm, n // tn),
        in_specs=[
            pl.BlockSpec((tm, d), lambda i, j: (i, 0)),
            pl.BlockSpec((1, d), lambda i, j: (0, 0)),
            pl.BlockSpec((d, tn), lambda i, j: (0, j)),
            pl.BlockSpec((d, LANES), lambda i, j: (0, 0)),
        ],
        out_specs=[
            pl.BlockSpec((tm, tn), lambda i, j: (i, j)),
            pl.BlockSpec((tm, LANES), lambda i, j: (i, 0)),
        ],
        out_shape=[jax.ShapeDtypeStruct((m, n), F32), jax.ShapeDtypeStruct((m, LANES), F32)],
        scratch_shapes=[pltpu.VMEM((tm, d), BF16)],
        compiler_params=_cparams(("parallel", "arbitrary")),
        name="inproj0",
    )(x, g, wm, wd)


def _ssd_kernel(xbc_ref, xnx_ref, dt_ref, cw_ref, cb_ref, dtb_ref, alog_ref, dexp_ref, e_ref, conv0_ref, h0_ref,
                y_ref, hout_ref, convout_ref,
                xpad_ref, halo_ref, act_ref, ht_ref, act_t_ref, ex_ref, *, rows_in, n_valid, lookahead):
    q = SSM_CHUNK
    gw = SSM_HPG * SSM_HEAD_DIM
    c = pl.program_id(1)
    last_c = pl.num_programs(1) - 1
    halo = SUBLANES
    tail0 = halo - (SSM_CONV - 1)
    cchunk = SSM_CONV_DIM // SSM_GROUPS
    cur = c % 2

    def conv_cols(slot, j0):
        cs = slice(j0, j0 + cchunk)
        sh = xpad_ref[:, cs]
        acc = cb_ref[:, cs] + sh[halo:] * cw_ref[SSM_CONV - 1:SSM_CONV, cs]
        for k in range(SSM_CONV - 2, -1, -1):
            sh = pltpu.roll(sh, 1, axis=0)
            acc = acc + sh[halo:] * cw_ref[k:k + 1, cs]
        act_ref[slot, :, cs] = _silu(acc)

    def tail_of(ref):
        return ref[n_valid - (SSM_CONV - 1):n_valid, :]

    @pl.when(c == 0)
    def _():
        halo_ref[...] = jnp.zeros(halo_ref.shape, F32)
        halo_ref[tail0:halo, :] = conv0_ref[0]
        xpad_ref[0:halo, :] = halo_ref[...]
        if rows_in < q:
            xpad_ref[halo + rows_in:halo + q, :] = jnp.zeros((q - rows_in, SSM_CONV_DIM), F32)
        xpad_ref[halo:halo + rows_in, :] = xbc_ref[...]
        for j0 in range(0, SSM_CONV_DIM, cchunk):
            conv_cols(0, j0)
        halo_ref[tail0:halo, :] = tail_of(xbc_ref)
        for g in range(SSM_GROUPS):
            ht_ref[g] = h0_ref[0, g * SSM_HPG:(g + 1) * SSM_HPG].reshape(gw, SSM_STATE).T

    @pl.when(c == last_c)
    def _():
        convout_ref[0] = halo_ref[tail0:halo, :]

    if lookahead:
        xpad_ref[0:halo, :] = halo_ref[...]
        xpad_ref[halo:halo + rows_in, :] = xnx_ref[...]

    row = lax.broadcasted_iota(jnp.int32, (q, LANES), 0)
    dt_raw = dt_ref[...]
    if rows_in < q:
        dt_raw = jnp.concatenate([dt_raw, jnp.zeros((q - rows_in, LANES), F32)], axis=0)
    dt = _softplus(dt_raw + dtb_ref[...])
    if n_valid < q:
        dt = jnp.where(row < n_valid, dt, 0.0)
    adt = dt * (-jnp.exp(alog_ref[...]))
    acum = adt
    k = 1
    while k < q:
        acum = acum + jnp.where(row >= k, pltpu.roll(acum, k, axis=0), 0.0)
        k *= 2
    act_t_ref[...] = acum.T
    eac = jnp.exp(acum)
    dec = jnp.exp(acum[q - 1:q, :] - acum)
    for i, val in enumerate((dt, dt * dec, eac)):
        hi, mid = _pieces2(val)
        ex = jnp.dot(jnp.concatenate([hi, mid], axis=0), e_ref[...], preferred_element_type=F32)
        ex_ref[i] = ex[:q] + ex[q:]

    r_i = lax.broadcasted_iota(jnp.int32, (q, q), 0)
    c_i = lax.broadcasted_iota(jnp.int32, (q, q), 1)
    causal = r_i >= c_i
    blk_r = lax.broadcasted_iota(jnp.int32, (SSM_HPG * q, gw), 0) // q
    blk_c = lax.broadcasted_iota(jnp.int32, (SSM_HPG * q, gw), 1) // SSM_HEAD_DIM
    diag_blocks = blk_r == blk_c

    gn = SSM_GROUPS * SSM_STATE
    for g in range(SSM_GROUPS):
        gs = slice(g * gw, (g + 1) * gw)
        if lookahead:
            conv_cols(1 - cur, g * cchunk)
        bt = act_ref[cur, :, SSM_D_INNER + g * SSM_STATE:SSM_D_INNER + (g + 1) * SSM_STATE].T.astype(BF16)
        c_g = act_ref[cur, :, SSM_D_INNER + gn + g * SSM_STATE:SSM_D_INNER + gn + (g + 1) * SSM_STATE].astype(BF16)
        cb = jnp.dot(c_g, bt, preferred_element_type=F32)
        mats = []
        for r in range(SSM_HPG):
            h = g * SSM_HPG + r
            seg = acum[:, h:h + 1] - act_t_ref[h:h + 1, :]
            mats.append((cb * jnp.exp(jnp.where(causal, seg, -jnp.inf))).astype(BF16))
        mcat = jnp.concatenate(mats, axis=1)
        xs_g = act_ref[cur, :, gs]
        xdt = (xs_g * ex_ref[0, :, gs]).astype(BF16)
        bd = jnp.where(diag_blocks, jnp.concatenate([xdt] * SSM_HPG, axis=0), jnp.zeros((), BF16))
        y_diag = jnp.dot(mcat, bd, preferred_element_type=F32)
        ht = ht_ref[g]
        y_off = jnp.dot(c_g, ht.astype(BF16), preferred_element_type=F32)
        eac_g = ex_ref[2, :, gs]
        y = y_diag + y_off * eac_g + xs_g * dexp_ref[:, gs]
        y_ref[:, gs] = y[:rows_in]
        xdtd = (xs_g * ex_ref[1, :, gs]).astype(BF16)
        ht_ref[g] = ht * eac_g[q - 1:q, :] + jnp.dot(bt, xdtd, preferred_element_type=F32)

    if lookahead:
        @pl.when(c < last_c)
        def _():
            halo_ref[tail0:halo, :] = tail_of(xnx_ref)

    @pl.when(c == last_c)
    def _():
        for g in range(SSM_GROUPS):
            hout_ref[0, g * SSM_HPG:(g + 1) * SSM_HPG] = ht_ref[g].T.reshape(SSM_HPG, SSM_HEAD_DIM, SSM_STATE)


def _ssd(om, od, cw, cb, dtb, alog, dexp, emat, conv0, h0, nb, nc, rows_in, n_valid):
    m = om.shape[0]
    gw = SSM_HPG * SSM_HEAD_DIM
    kern = functools.partial(_ssd_kernel, rows_in=rows_in, n_valid=n_valid, lookahead=nc > 1)
    vec = lambda: pl.BlockSpec((1, LANES), lambda b, c: (0, 0))
    return pl.pallas_call(
        kern,
        grid=(nb, nc),
        in_specs=[
            pl.BlockSpec((rows_in, SSM_CONV_DIM), lambda b, c: (b * nc + c, 0)),
            pl.BlockSpec((rows_in, SSM_CONV_DIM), lambda b, c: (b * nc + jnp.minimum(c + 1, nc - 1), 0)),
            pl.BlockSpec((rows_in, LANES), lambda b, c: (b * nc + c, 0)),
            pl.BlockSpec((SSM_CONV, SSM_CONV_DIM), lambda b, c: (0, 0)),
            pl.BlockSpec((1, SSM_CONV_DIM), lambda b, c: (0, 0)),
            vec(), vec(),
            pl.BlockSpec((1, SSM_D_INNER), lambda b, c: (0, 0)),
            pl.BlockSpec((LANES, SSM_D_INNER), lambda b, c: (0, 0)),
            pl.BlockSpec((1, SSM_CONV - 1, SSM_CONV_DIM), lambda b, c: (b, 0, 0)),
            pl.BlockSpec((1, SSM_HEADS, SSM_HEAD_DIM, SSM_STATE), lambda b, c: (b, 0, 0, 0)),
        ],
        out_specs=[
            pl.BlockSpec((rows_in, SSM_D_INNER), lambda b, c: (b * nc + c, 0)),
            pl.BlockSpec((1, SSM_HEADS, SSM_HEAD_DIM, SSM_STATE), lambda b, c: (b, 0, 0, 0)),
            pl.BlockSpec((1, SSM_CONV - 1, SSM_CONV_DIM), lambda b, c: (b, 0, 0)),
        ],
        out_shape=[
            jax.ShapeDtypeStruct((m, SSM_D_INNER), F32),
            jax.ShapeDtypeStruct((nb, SSM_HEADS, SSM_HEAD_DIM, SSM_STATE), F32),
            jax.ShapeDtypeStruct((nb, SSM_CONV - 1, SSM_CONV_DIM), F32),
        ],
        scratch_shapes=[
            pltpu.VMEM((SUBLANES + SSM_CHUNK, SSM_CONV_DIM), F32),
            pltpu.VMEM((SUBLANES, SSM_CONV_DIM), F32),
            pltpu.VMEM((2, SSM_CHUNK, SSM_CONV_DIM), F32),
            pltpu.VMEM((SSM_GROUPS, SSM_STATE, gw), F32),
            pltpu.VMEM((LANES, SSM_CHUNK), F32),
            pltpu.VMEM((3, SSM_CHUNK, SSM_D_INNER), F32),
        ],
        compiler_params=_cparams(("arbitrary", "arbitrary")),
        name="ssd",
    )(om, om, od, cw, cb, dtb, alog, dexp, emat, conv0, h0)


def _outproj0_kernel(y_ref, z_ref, nw_ref, w_ref, np_ref, x_ref, o_ref):
    h = y_ref[...] * _silu(z_ref[...])
    gw = SSM_D_INNER // SSM_GROUPS
    hn = jnp.concatenate([_rms(h[:, g * gw:(g + 1) * gw]) for g in range(SSM_GROUPS)], axis=1)
    hn = (hn * nw_ref[...]).astype(BF16)
    o = jnp.dot(hn, w_ref[...], preferred_element_type=F32)
    o_ref[...] = x_ref[...] + _rms(o) * np_ref[...]


def _outproj0(y, om, nw, w, npost, x, tm):
    m = y.shape[0]
    zblk = SSM_CONV_DIM // SSM_D_INNER
    return pl.pallas_call(
        _outproj0_kernel,
        grid=(m // tm,),
        in_specs=[
            pl.BlockSpec((tm, SSM_D_INNER), lambda i: (i, 0)),
            pl.BlockSpec((tm, SSM_D_INNER), lambda i: (i, zblk)),
            pl.BlockSpec((1, SSM_D_INNER), lambda i: (0, 0)),
            pl.BlockSpec((SSM_D_INNER, D_MODEL), lambda i: (0, 0)),
            pl.BlockSpec((1, D_MODEL), lambda i: (0, 0)),
            pl.BlockSpec((tm, D_MODEL), lambda i: (i, 0)),
        ],
        out_specs=pl.BlockSpec((tm, D_MODEL), lambda i: (i, 0)),
        out_shape=jax.ShapeDtypeStruct((m, D_MODEL), F32),
        compiler_params=_cparams(("parallel",)),
        name="outproj0",
    )(y, om, nw, w, npost, x)


def _inproj1p_kernel(x_ref, g_ref, wt_ref, wft_ref, bf_ref, qt_ref, kt_ref, vt_ref, vtb_ref, zt_ref, lft_ref):
    un = (_rms(x_ref[...]) * g_ref[...]).astype(BF16)
    w = ATTN_WIDTH
    proj = lambda i: lax.dot_general(wt_ref[i * w:(i + 1) * w, :], un, _NT, preferred_element_type=F32)
    qt_ref[0] = (proj(0) * (ATTN_HEAD_DIM ** -0.5 * LOG2E)).astype(BF16)
    kt_ref[0] = proj(1)
    vt = proj(2)
    vt_ref[0] = vt
    vtb_ref[0] = vt.astype(BF16)
    zt_ref[0] = proj(3)
    ft = lax.dot_general(wft_ref[...], un, _NT, preferred_element_type=F32)
    lft_ref[0] = _log_sigmoid(ft + bf_ref[...])


def _inproj1p(x, g, wt, wft, bfc, nb, seq, tm):
    m, d = x.shape
    nq = seq // tm
    w = ATTN_WIDTH
    full = lambda r, c: pl.BlockSpec((r, c), lambda b, i: (0, 0))
    feat = lambda r: pl.BlockSpec((1, r, tm), lambda b, i: (b, 0, i))
    fshape = lambda dt: jax.ShapeDtypeStruct((nb, w, seq), dt)
    return pl.pallas_call(
        _inproj1p_kernel,
        grid=(nb, nq),
        in_specs=[pl.BlockSpec((tm, d), lambda b, i: (b * nq + i, 0)), full(1, d), full(4 * w, d),
                  full(ATTN_HEADS, d), full(ATTN_HEADS, 1)],
        out_specs=[feat(w), feat(w), feat(w), feat(w), feat(w), feat(ATTN_HEADS)],
        out_shape=[fshape(BF16), fshape(F32), fshape(F32), fshape(BF16), fshape(F32),
                   jax.ShapeDtypeStruct((nb, ATTN_HEADS, seq), F32)],
        compiler_params=_cparams(("parallel", "parallel")),
        name="inproj1p",
    )(x, g, wt, wft, bfc)


def _inproj1s_kernel(x_ref, g_ref, w_ref, wf_ref, bf_ref, o_ref, lf_ref):
    un = (_rms(x_ref[...]) * g_ref[...]).astype(BF16)
    o_ref[...] = jnp.dot(un, w_ref[...], preferred_element_type=F32)
    lf_ref[...] = _log_sigmoid(jnp.dot(un, wf_ref[...], preferred_element_type=F32) + bf_ref[...])


def _inproj1s(x, g, w, wf, bfr):
    m, d = x.shape
    n = w.shape[1]
    return pl.pallas_call(
        _inproj1s_kernel,
        grid=(1,),
        in_specs=[pl.BlockSpec((m, d), lambda i: (0, 0)), pl.BlockSpec((1, d), lambda i: (0, 0)),
                  pl.BlockSpec((d, n), lambda i: (0, 0)), pl.BlockSpec((d, LANES), lambda i: (0, 0)),
                  pl.BlockSpec((1, LANES), lambda i: (0, 0))],
        out_specs=[pl.BlockSpec((m, n), lambda i: (0, 0)), pl.BlockSpec((m, LANES), lambda i: (0, 0))],
        out_shape=[jax.ShapeDtypeStruct((m, n), F32), jax.ShapeDtypeStruct((m, LANES), F32)],
        compiler_params=_cparams(("arbitrary",)),
        name="inproj1s",
    )(x, g, w, wf, bfr)


def _cumsum_kernel(x_ref, o_ref):
    nblk = x_ref.shape[2] // LANES
    upper = _tri(LANES, lambda r, c: r <= c)

    def body(j, carry):
        off = pl.multiple_of(j * LANES, LANES)
        loc = _dot3(x_ref[0, :, pl.ds(off, LANES)], upper)
        o_ref[0, :, pl.ds(off, LANES)] = loc + carry
        return carry + loc[:, LANES - 1:LANES]

    lax.fori_loop(0, nblk, body, jnp.zeros((x_ref.shape[1], 1), F32))


def _cumsum(x):
    nb, h, seq = x.shape
    return pl.pallas_call(
        _cumsum_kernel,
        grid=(nb,),
        in_specs=[pl.BlockSpec((1, h, seq), lambda b: (b, 0, 0))],
        out_specs=pl.BlockSpec((1, h, seq), lambda b: (b, 0, 0)),
        out_shape=jax.ShapeDtypeStruct((nb, h, seq), F32),
        compiler_params=_cparams(("parallel",)),
        name="lf_cumsum",
    )(x)


def _bias_rows(c2, first, n, width):
    hi, mid, lo = (p.astype(F32) for p in _split3(c2))
    row = lax.broadcasted_iota(jnp.int32, (n, width), 0)
    ones_first = 3 - first
    out = jnp.where((row >= ones_first) & (row < ones_first + 3), 1.0, 0.0)
    for i, piece in enumerate((hi, mid, lo)):
        out = jnp.where(row == first + i, piece, out)
    return out


def _kaug_kernel(kt_ref, c_ref, o_ref, *, chunk):
    h = pl.program_id(1)

    def body(j, carry):
        off = pl.multiple_of(j * chunk, chunk)
        c2 = c_ref[0, pl.ds(h, 1), pl.ds(off, chunk)] * LOG2E
        full = jnp.concatenate([kt_ref[0, :, pl.ds(off, chunk)],
                                _bias_rows(-c2, 3, ATTN_HEAD_DIM, chunk)], axis=0)
        o_ref[0, 0, pl.ds(off, chunk), :] = full.T.astype(BF16)
        return carry

    lax.fori_loop(0, kt_ref.shape[2] // chunk, body, 0)


def _kaug(kt, c, chunk):
    nb, w, seq = kt.shape
    return pl.pallas_call(
        functools.partial(_kaug_kernel, chunk=chunk),
        grid=(nb, ATTN_HEADS),
        in_specs=[pl.BlockSpec((1, ATTN_HEAD_DIM, seq), lambda b, h: (b, h, 0)),
                  pl.BlockSpec((1, ATTN_HEADS, seq), lambda b, h: (b, 0, 0))],
        out_specs=pl.BlockSpec((1, 1, seq, LANES), lambda b, h: (b, h, 0, 0)),
        out_shape=jax.ShapeDtypeStruct((nb, ATTN_HEADS, seq, LANES), BF16),
        compiler_params=_cparams(("parallel", "parallel")),
        name="fox_kaug",
    )(kt, c)


def _flash_body(qi, kj, qt_ref, ka_ref, vt_ref, cq_ref, o_ref, qa_ref, m_ref, acc_ref, s_ref, *, tq, tk):
    hd = ATTN_HEAD_DIM

    @pl.when(kj == 0)
    def _():
        m_ref[...] = jnp.full(m_ref.shape, NEG_BIG, F32)
        acc_ref[...] = jnp.zeros(acc_ref.shape, F32)

        def build(h, carry):
            off = pl.multiple_of(h * hd, hd)
            c2 = cq_ref[0, pl.ds(h, 1), :] * LOG2E
            qa_ref[h, 0:hd, :] = qt_ref[0, pl.ds(off, hd), :]
            qa_ref[h, hd:2 * hd, :] = _bias_rows(c2, 0, hd, tq).astype(BF16)
            return carry

        lax.fori_loop(0, ATTN_HEADS, build, 0)

    def step(masked):
        if masked:
            key = kj * tk + lax.broadcasted_iota(jnp.int32, (tk, tq), 0)
            qry = qi * tq + lax.broadcasted_iota(jnp.int32, (tk, tq), 1)
            keep = key <= qry
        ones = jnp.ones((AUG, tk), BF16)

        def scores(h, slot):
            st = jnp.dot(ka_ref[0, h], qa_ref[h], preferred_element_type=F32)
            if masked:
                st = jnp.where(keep, st, NEG_BIG)
            s_ref[slot] = st
            return jnp.max(st, axis=0, keepdims=True)

        def softmax_pv(h, slot, mx):
            off = pl.multiple_of(h * hd, hd)
            m_prev = m_ref[h]
            m_new = jnp.maximum(m_prev, mx)
            pt = jnp.exp2(s_ref[slot] - m_new[0:1]).astype(BF16)
            va = jnp.concatenate([vt_ref[0, pl.ds(off, hd), :], ones], axis=0)
            pv = jnp.dot(va, pt, preferred_element_type=F32)
            acc_ref[h] = acc_ref[h] * jnp.exp2(m_prev - m_new)[0:1] + pv
            m_ref[h] = m_new

        def run(h0, n, mx):
            for k in range(n):
                nxt = scores(h0 + k + 1, (k + 1) % 2)
                softmax_pv(h0 + k, k % 2, mx)
                mx = nxt
            return mx

        per_iter = FLASH_HEADS_PER_ITER
        n_iter = ATTN_HEADS // per_iter - 1
        mx = lax.fori_loop(0, n_iter, lambda i, mx: run(i * per_iter, per_iter, mx), scores(0, 0))
        mx = run(n_iter * per_iter, per_iter - 1, mx)
        softmax_pv(ATTN_HEADS - 1, (per_iter - 1) % 2, mx)

    @pl.when(kj < qi)
    def _():
        step(False)

    @pl.when(kj == qi)
    def _():
        step(True)

        def fin(h, carry):
            off = pl.multiple_of(h * hd, hd)
            a = acc_ref[h]
            o_ref[0, pl.ds(off, hd), :] = a[0:hd] / a[hd:hd + 1]
            return carry

        lax.fori_loop(0, ATTN_HEADS, fin, 0)


def _decode_body(j, last_j, q_ref, kn_ref, vn_ref, lfn_ref, k_refs, v_refs, lf_refs,
                 o_ref, qg_ref, m_ref, l_ref, acc_ref, carry_ref, *, lq):
    pps = len(k_refs)
    nrow = ATTN_HEADS * lq
    hd = ATTN_HEAD_DIM
    ngrp = DECODE_HEAD_GROUPS
    hpg = ATTN_HEADS // ngrp
    grow = hpg * lq
    gcol = hpg * hd
    rsl = lambda g: slice(g * grow, (g + 1) * grow)
    csl = lambda g: slice(g * gcol, (g + 1) * gcol)

    @pl.when(j == 0)
    def _():
        rh = lax.broadcasted_iota(jnp.int32, (grow, gcol), 0) // lq
        ch = lax.broadcasted_iota(jnp.int32, (grow, gcol), 1) // hd
        for g in range(ngrp):
            qt = jnp.concatenate([q_ref[0, :, csl(g)] * (hd ** -0.5)] * hpg, axis=0)
            qg_ref[g] = jnp.where(rh == ch, qt, 0.0).astype(BF16)
        m_ref[...] = jnp.full(m_ref.shape, NEG_BIG, F32)
        l_ref[...] = jnp.zeros(l_ref.shape, F32)
        acc_ref[...] = jnp.zeros(acc_ref.shape, F32)
        carry_ref[...] = jnp.zeros(carry_ref.shape, F32)

    def expand_heads(r):
        return jnp.concatenate(
            [jnp.broadcast_to(r[h:h + 1, :], (lq, r.shape[1])) for h in range(ATTN_HEADS)], axis=0)

    def merge(s_list, pv_fn):
        m_prev = m_ref[...]
        m_new = m_prev
        for s in s_list:
            m_new = jnp.maximum(m_new, jnp.max(s, axis=1, keepdims=True))
        alpha = jnp.exp(m_prev - m_new)
        lsum = jnp.zeros_like(m_prev)
        pv = [None] * ngrp
        for i, s in enumerate(s_list):
            p = jnp.exp(s - m_new)
            lsum = lsum + jnp.sum(p, axis=1, keepdims=True)
            pb = p.astype(BF16)
            for g in range(ngrp):
                d = pv_fn(i, g, pb[rsl(g)])
                pv[g] = d if pv[g] is None else pv[g] + d
        l_ref[...] = alpha * l_ref[...] + lsum
        for g in range(ngrp):
            acc_ref[g] = alpha[rsl(g)] * acc_ref[g] + pv[g]
        m_ref[...] = m_new

    later = _tri(PAGE_SIZE, lambda r, c: r > c)
    carry = carry_ref[...]
    for p0 in range(0, pps, DECODE_PAGES_PER_MERGE):
        s_list, v_list = [], []
        for p in range(p0, min(p0 + DECODE_PAGES_PER_MERGE, pps)):
            lf = lf_refs[p][0]
            r = _dot3(lf, later) + carry
            carry = carry + jnp.sum(lf, axis=1, keepdims=True)
            kt = k_refs[p][0].astype(BF16)
            s = jnp.concatenate([jnp.dot(qg_ref[g], kt[csl(g)], preferred_element_type=F32)
                                 for g in range(ngrp)], axis=0)
            s_list.append(s + expand_heads(r))
            v_list.append(v_refs[p][0].astype(BF16))
        merge(s_list, lambda i, g, pr, v_list=v_list: lax.dot_general(
            pr, v_list[i][csl(g)], _NT, preferred_element_type=F32))
    carry_ref[...] = carry

    @pl.when(j == last_j)
    def _():
        pad = jnp.zeros((LANES - lq, ATTN_WIDTH), F32)
        kn = jnp.concatenate([kn_ref[0], pad], axis=0).astype(BF16)
        vn = jnp.concatenate([vn_ref[0], pad], axis=0).astype(BF16)
        s = jnp.concatenate([lax.dot_general(qg_ref[g], kn[:, csl(g)], _NT, preferred_element_type=F32)
                             for g in range(ngrp)], axis=0)
        cn = _dot3(lfn_ref[0], _tri(LANES, lambda r, c: r <= c))
        s = s - expand_heads(cn)
        qpos = lax.broadcasted_iota(jnp.int32, (nrow, LANES), 0) % lq
        kpos = lax.broadcasted_iota(jnp.int32, (nrow, LANES), 1)
        s = jnp.where(kpos <= qpos, s, NEG_BIG)
        merge([s], lambda i, g, pr: jnp.dot(pr, vn[:, csl(g)], preferred_element_type=F32))
        for g in range(ngrp):
            out = acc_ref[g] / l_ref[rsl(g), :]
            for hl in range(hpg):
                h = g * hpg + hl
                o_ref[0, :, h * hd:(h + 1) * hd] = out[hl * lq:(hl + 1) * lq, hl * hd:(hl + 1) * hd]


def _attn_kernel(bt_ref, qit_ref, kjt_ref, pt_ref, qt_ref, ka_ref, vt_ref, cq_ref,
                 qs_ref, kn_ref, vn_ref, lfn_ref, *refs, tq, tk, pps, lq, n_dec, spe):
    del bt_ref, pt_ref
    k_refs = refs[0:pps]
    v_refs = refs[pps:2 * pps]
    lf_refs = refs[2 * pps:3 * pps]
    (o_ref, os_ref, qa_ref, m_ref, acc_ref, s_ref, qg_ref, md_ref, ld_ref, accd_ref, carry_ref) = refs[3 * pps:]
    t = pl.program_id(0)
    _flash_body(qit_ref[t], kjt_ref[t], qt_ref, ka_ref, vt_ref, cq_ref, o_ref, qa_ref, m_ref, acc_ref, s_ref,
                tq=tq, tk=tk)

    @pl.when(t < n_dec)
    def _():
        _decode_body(t % spe, spe - 1, qs_ref, kn_ref, vn_ref, lfn_ref, k_refs, v_refs, lf_refs,
                     os_ref, qg_ref, md_ref, ld_ref, accd_ref, carry_ref, lq=lq)


def _attention(qt, ka, vtb, c, q_s, kn, vn, lfn_t, ck, cv, clf, page_table, tq, pps):
    nb, w, seq = qt.shape
    nbs, lq, _ = q_s.shape
    npages = page_table.shape[1]
    nq = seq // tq
    tri = [(b, i, j) for b in range(nb) for i in range(nq) for j in range(i + 1)]
    nsteps = len(tri)
    spe = npages // pps
    n_dec = nbs * spe
    assert n_dec <= nsteps and npages % pps == 0, (n_dec, nsteps)
    bt, qit, kjt = (jnp.asarray([x[k] for x in tri], jnp.int32) for k in range(3))
    nrow = ATTN_HEADS * lq
    ng = DECODE_HEAD_GROUPS
    kern = functools.partial(_attn_kernel, tq=tq, tk=tq, pps=pps, lq=lq, n_dec=n_dec, spe=spe)

    def dec_elem(t):
        return jnp.minimum(t // spe, nbs - 1)

    def page_spec(shape, p):
        def imap(t, bt, qit, kjt, pt):
            td = jnp.minimum(t, n_dec - 1)
            return (pt[td // spe, npages - 1 - ((td % spe) * pps + p)], 0, 0)
        return pl.BlockSpec(shape, imap)

    tokspec = lambda: pl.BlockSpec((1, lq, w), lambda t, bt, qit, kjt, pt: (dec_elem(t), 0, 0))
    in_specs = [
        pl.BlockSpec((1, w, tq), lambda t, bt, qit, kjt, pt: (bt[t], 0, qit[t])),
        pl.BlockSpec((1, ATTN_HEADS, tq, LANES), lambda t, bt, qit, kjt, pt: (bt[t], 0, kjt[t], 0)),
        pl.BlockSpec((1, w, tq), lambda t, bt, qit, kjt, pt: (bt[t], 0, kjt[t])),
        pl.BlockSpec((1, ATTN_HEADS, tq), lambda t, bt, qit, kjt, pt: (bt[t], 0, qit[t])),
        tokspec(), tokspec(), tokspec(),
        pl.BlockSpec((1, ATTN_HEADS, LANES), lambda t, bt, qit, kjt, pt: (dec_elem(t), 0, 0)),
    ]
    in_specs += [page_spec((1, w, PAGE_SIZE), p) for p in range(pps)]
    in_specs += [page_spec((1, w, PAGE_SIZE), p) for p in range(pps)]
    in_specs += [page_spec((1, ATTN_HEADS, PAGE_SIZE), p) for p in range(pps)]
    return pl.pallas_call(
        kern,
        grid_spec=pltpu.PrefetchScalarGridSpec(
            num_scalar_prefetch=4,
            grid=(nsteps,),
            in_specs=in_specs,
            out_specs=[
                pl.BlockSpec((1, w, tq), lambda t, bt, qit, kjt, pt: (bt[t], 0, qit[t])),
                pl.BlockSpec((1, lq, w), lambda t, bt, qit, kjt, pt: (dec_elem(t), 0, 0)),
            ],
            scratch_shapes=[
                pltpu.VMEM((ATTN_HEADS, 2 * ATTN_HEAD_DIM, tq), BF16),
                pltpu.VMEM((ATTN_HEADS, SUBLANES, tq), F32),
                pltpu.VMEM((ATTN_HEADS, ATTN_HEAD_DIM + AUG, tq), F32),
                pltpu.VMEM((2, tq, tq), F32),
                pltpu.VMEM((ng, nrow // ng, w // ng), BF16),
                pltpu.VMEM((nrow, 1), F32),
                pltpu.VMEM((nrow, 1), F32),
                pltpu.VMEM((ng, nrow // ng, w // ng), F32),
                pltpu.VMEM((ATTN_HEADS, 1), F32),
            ],
        ),
        out_shape=[jax.ShapeDtypeStruct((nb, w, seq), F32), jax.ShapeDtypeStruct((nbs, lq, w), F32)],
        compiler_params=pltpu.CompilerParams(dimension_semantics=("arbitrary",), vmem_limit_bytes=ATTN_VMEM_LIMIT),
        name="fox_attention",
    )(bt, qit, kjt, page_table, qt, ka, vtb, c, q_s, kn, vn, lfn_t, *([ck] * pps), *([cv] * pps), *([clf] * pps))


def _outproj1_kernel(o_ref, z_ref, w_ref, np_ref, x_ref, out_ref):
    g = (o_ref[...] * _silu(z_ref[...])).astype(BF16)
    y = jnp.dot(g, w_ref[...], preferred_element_type=F32)
    out_ref[...] = x_ref[...] + _rms(y) * np_ref[...]


def _outproj1(o, z, zblk, w, npost, x, tm):
    m = o.shape[0]
    return pl.pallas_call(
        _outproj1_kernel,
        grid=(m // tm,),
        in_specs=[
            pl.BlockSpec((tm, ATTN_WIDTH), lambda i: (i, 0)),
            pl.BlockSpec((tm, ATTN_WIDTH), lambda i: (i, zblk)),
            pl.BlockSpec((ATTN_WIDTH, D_MODEL), lambda i: (0, 0)),
            pl.BlockSpec((1, D_MODEL), lambda i: (0, 0)),
            pl.BlockSpec((tm, D_MODEL), lambda i: (i, 0)),
        ],
        out_specs=pl.BlockSpec((tm, D_MODEL), lambda i: (i, 0)),
        out_shape=jax.ShapeDtypeStruct((m, D_MODEL), F32),
        compiler_params=_cparams(("parallel",)),
        name="outproj1",
    )(o, z, w, npost, x)


def _outproj1p_kernel(ot_ref, zt_ref, w_ref, np_ref, x_ref, out_ref):
    gt = (ot_ref[0] * _silu(zt_ref[0])).astype(BF16)
    y = lax.dot_general(gt, w_ref[...], _TN, preferred_element_type=F32)
    out_ref[...] = x_ref[...] + _rms(y) * np_ref[...]


def _outproj1p(ot, zt, w, npost, x, tm):
    nb, wd, seq = ot.shape
    nq = seq // tm
    feat = lambda: pl.BlockSpec((1, wd, tm), lambda b, i: (b, 0, i))
    return pl.pallas_call(
        _outproj1p_kernel,
        grid=(nb, nq),
        in_specs=[feat(), feat(),
                  pl.BlockSpec((wd, D_MODEL), lambda b, i: (0, 0)),
                  pl.BlockSpec((1, D_MODEL), lambda b, i: (0, 0)),
                  pl.BlockSpec((tm, D_MODEL), lambda b, i: (b * nq + i, 0))],
        out_specs=pl.BlockSpec((tm, D_MODEL), lambda b, i: (b * nq + i, 0)),
        out_shape=jax.ShapeDtypeStruct((nb * seq, D_MODEL), F32),
        compiler_params=_cparams(("parallel", "parallel")),
        name="outproj1p",
    )(ot, zt, w, npost, x)


def _pad_lanes(v):
    return jnp.pad(v.astype(F32), (0, LANES - v.shape[0])).reshape(1, LANES)


def kernel(x_prompt, x_sample, state_ssm, state_conv, cache_k, cache_v, cache_logf, page_table, norm_pre, norm_post, ssm_w_in, ssm_conv_w, ssm_conv_b, ssm_dt_bias, ssm_a_log, ssm_d, ssm_norm_w, ssm_w_out, attn_w_in, attn_b_f, attn_w_out):
    nbp, seq, d = x_prompt.shape
    nbs, lq, _ = x_sample.shape
    xp = x_prompt.reshape(nbp * seq, d)
    xs = x_sample.reshape(nbs * lq, d)
    tm_p = min(1024, seq)

    w_in = ssm_w_in[0]
    z_end = SSM_D_INNER
    xbc_end = SSM_D_INNER + SSM_CONV_DIM
    wm = jnp.concatenate([w_in[:, z_end:xbc_end], w_in[:, :z_end]], axis=1).astype(BF16)
    wd = jnp.pad(w_in[:, xbc_end:], ((0, 0), (0, LANES - SSM_HEADS))).astype(BF16)
    g0 = norm_pre[0:1]
    head_of_lane = jnp.arange(SSM_D_INNER, dtype=jnp.int32) // SSM_HEAD_DIM
    expand = (jnp.arange(LANES, dtype=jnp.int32)[:, None] == head_of_lane[None, :]).astype(BF16)
    d_lanes = jnp.repeat(ssm_d[0].astype(F32), SSM_HEAD_DIM).reshape(1, SSM_D_INNER)
    ssd_vecs = (ssm_conv_w[0], ssm_conv_b[0:1], _pad_lanes(ssm_dt_bias[0]), _pad_lanes(ssm_a_log[0]),
                d_lanes, expand)
    w_out0 = ssm_w_out[0].astype(BF16)
    nw0 = ssm_norm_w[0:1]
    np0 = norm_post[0:1]

    om_p, od_p = _inproj0(xp, g0, wm, wd, tm_p)
    om_s, od_s = _inproj0(xs, g0, wm, wd, nbs * lq)

    conv0_p = jnp.zeros((nbp, SSM_CONV - 1, SSM_CONV_DIM), F32)
    h0_p = jnp.zeros((nbp, SSM_HEADS, SSM_HEAD_DIM, SSM_STATE), F32)
    y_p, ssm_p, conv_p = _ssd(om_p, od_p, *ssd_vecs, conv0_p, h0_p, nbp, seq // SSM_CHUNK, SSM_CHUNK, SSM_CHUNK)
    y_s, ssm_s, conv_s = _ssd(om_s, od_s, *ssd_vecs, state_conv[0], state_ssm[0], nbs, 1, lq, lq)

    hp1 = _outproj0(y_p, om_p, nw0, w_out0, np0, xp, min(512, seq))
    hs1 = _outproj0(y_s, om_s, nw0, w_out0, np0, xs, nbs * lq)

    wa = attn_w_in[0]
    w = ATTN_WIDTH
    wa_b = wa[:, :4 * w].astype(BF16)
    wa_t = wa.T.astype(BF16)
    wf = jnp.pad(wa[:, 4 * w:], ((0, 0), (0, LANES - ATTN_HEADS))).astype(BF16)
    g1 = norm_pre[1:2]
    np1 = norm_post[1:2]
    w_out1 = attn_w_out[0].astype(BF16)

    qt, kt, vt, vtb, zt, lft = _inproj1p(
        hp1, g1, wa_t[:4 * w], wa_t[4 * w:], attn_b_f[0].astype(F32).reshape(ATTN_HEADS, 1),
        nbp, seq, min(512, seq))
    c_p = _cumsum(lft)
    ka = _kaug(kt, c_p, min(1024, seq))
    tq = min(512, seq)

    proj_s, lf_s = _inproj1s(hs1, g1, wa_b, wf, _pad_lanes(attn_b_f[0]))
    q_s = proj_s[:, :w].reshape(nbs, lq, w)
    k_s = proj_s[:, w:2 * w].reshape(nbs, lq, w)
    v_s = proj_s[:, 2 * w:3 * w].reshape(nbs, lq, w)
    lf_s = lf_s[:, :ATTN_HEADS].reshape(nbs, lq, ATTN_HEADS)
    lfn_t = jnp.pad(lf_s.transpose(0, 2, 1), ((0, 0), (0, 0), (0, LANES - lq)))
    npool = cache_k.shape[1]
    ck = cache_k[0].transpose(0, 2, 3, 1).reshape(npool, w, PAGE_SIZE)
    cv = cache_v[0].transpose(0, 2, 3, 1).reshape(npool, w, PAGE_SIZE)
    clf = cache_logf[0].transpose(0, 2, 1)
    ot, o_s = _attention(qt, ka, vtb, c_p, q_s, k_s, v_s, lfn_t, ck, cv, clf, page_table, tq,
                         min(DECODE_PAGES_PER_STEP, page_table.shape[1]))
    y_prompt = _outproj1p(ot, zt, w_out1, np1, hp1, min(512, seq))
    y_sample = _outproj1(o_s.reshape(nbs * lq, w), proj_s, 3, w_out1, np1, hs1, nbs * lq)

    hd = ATTN_HEAD_DIM
    return (
        y_prompt.reshape(nbp, seq, d),
        y_sample.reshape(nbs, lq, d),
        kt.reshape(nbp, ATTN_HEADS, hd, seq).transpose(0, 3, 1, 2)[None],
        vt.reshape(nbp, ATTN_HEADS, hd, seq).transpose(0, 3, 1, 2)[None],
        lft.transpose(0, 2, 1)[None],
        k_s.reshape(1, nbs, lq, ATTN_HEADS, hd),
        v_s.reshape(1, nbs, lq, ATTN_HEADS, hd),
        lf_s[None],
        ssm_p[None],
        conv_p[None],
        ssm_s[None],
        conv_s[None],
    )
```

```python
import functools

import jax
import jax.numpy as jnp
from jax import lax
from jax.experimental import pallas as pl
from jax.experimental.pallas import tpu as pltpu

F32 = jnp.float32
BF16 = jnp.bfloat16

D_MODEL = 1024
SSM_D_INNER = 2048
SSM_HEAD_DIM = 64
SSM_HEADS = 32
SSM_GROUPS = 8
SSM_HPG = SSM_HEADS // SSM_GROUPS
SSM_STATE = 128
SSM_CONV = 4
SSM_CONV_DIM = SSM_D_INNER + 2 * SSM_GROUPS * SSM_STATE
SSM_CHUNK = 128
ATTN_HEADS = 16
ATTN_HEAD_DIM = 64
ATTN_WIDTH = ATTN_HEADS * ATTN_HEAD_DIM
PAGE_SIZE = 128
RMS_EPS = 1e-6
LANES = 128
SUBLANES = 8
NEG_BIG = -1e30
LOG2E = 1.4426950408889634
AUG = 16
DECODE_PAGES_PER_STEP = 16
DECODE_PAGES_PER_MERGE = 8
ATTN_VMEM_LIMIT = 60 * 1024 * 1024
DECODE_HEAD_GROUPS = 4
VMEM_LIMIT = 56 * 1024 * 1024

_NT = (((1,), (1,)), ((), ()))
_TN = (((0,), (0,)), ((), ()))


def _silu(x):
    return x / (1.0 + jnp.exp(-x))


def _softplus(x):
    return jnp.maximum(x, 0.0) + jnp.log1p(jnp.exp(-jnp.abs(x)))


def _log_sigmoid(x):
    return jnp.minimum(x, 0.0) - jnp.log1p(jnp.exp(-jnp.abs(x)))


def _rms(x):
    return x * lax.rsqrt(jnp.mean(x * x, axis=-1, keepdims=True) + RMS_EPS)


def _split3(x):
    hi = x.astype(BF16)
    r1 = x - hi.astype(F32)
    mid = r1.astype(BF16)
    lo = (r1 - mid.astype(F32)).astype(BF16)
    return hi, mid, lo


def _pieces2(x):
    hi = x.astype(BF16)
    return hi, (x - hi.astype(F32)).astype(BF16)


def _dot3(x, tri):
    n = x.shape[0]
    hi, mid, lo = _split3(x)
    r = jnp.dot(jnp.concatenate([hi, mid, lo], axis=0), tri, preferred_element_type=F32)
    return r[:n] + r[n:2 * n] + r[2 * n:]


def _tri(n, fn):
    r = lax.broadcasted_iota(jnp.int32, (n, n), 0)
    c = lax.broadcasted_iota(jnp.int32, (n, n), 1)
    return jnp.where(fn(r, c), 1.0, 0.0).astype(BF16)


def _cparams(sem, **kw):
    return pltpu.CompilerParams(dimension_semantics=sem, vmem_limit_bytes=VMEM_LIMIT, **kw)


def _inproj0_kernel(x_ref, g_ref, wm_ref, wd_ref, om_ref, od_ref, un_ref):
    @pl.when(pl.program_id(1) == 0)
    def _():
        un = (_rms(x_ref[...]) * g_ref[...]).astype(BF16)
        un_ref[...] = un
        od_ref[...] = jnp.dot(un, wd_ref[...], preferred_element_type=F32)

    om_ref[...] = jnp.dot(un_ref[...], wm_ref[...], preferred_element_type=F32)


def _inproj0(x, g, wm, wd, tm, tn=2048):
    m, d = x.shape
    n = wm.shape[1]
    return pl.pallas_call(
        _inproj0_kernel,
        grid=(m // tm, n // tn),
        in_specs=[
            pl.BlockSpec((tm, d), lambda i, j: (i, 0)),
            pl.BlockSpec((1, d), lambda i, j: (0, 0)),
            pl.BlockSpec((d, tn), lambda i, j: (0, j)),
            pl.BlockSpec((d, LANES), lambda i, j: (0, 0)),
        ],
        out_specs=[
            pl.BlockSpec((tm, tn), lambda i, j: (i, j)),
            pl.BlockSpec((tm, LANES), lambda i, j: (i, 0)),
        ],
        out_shape=[jax.ShapeDtypeStruct((m, n), F32), jax.ShapeDtypeStruct((m, LANES), F32)],
        scratch_shapes=[pltpu.VMEM((tm, d), BF16)],
        compiler_params=_cparams(("parallel", "arbitrary")),
        name="inproj0",
    )(x, g, wm, wd)


def _ssd_kernel(xbc_ref, xnx_ref, dt_ref, cw_ref, cb_ref, dtb_ref, alog_ref, dexp_ref, e_ref, conv0_ref, h0_ref,
                y_ref, hout_ref, convout_ref,
                xpad_ref, halo_ref, act_ref, ht_ref, act_t_ref, ex_ref, *, rows_in, n_valid, lookahead):
    q = SSM_CHUNK
    gw = SSM_HPG * SSM_HEAD_DIM
    c = pl.program_id(1)
    last_c = pl.num_programs(1) - 1
    halo = SUBLANES
    tail0 = halo - (SSM_CONV - 1)
    cchunk = SSM_CONV_DIM // SSM_GROUPS
    cur = c % 2

    def conv_cols(slot, j0):
        cs = slice(j0, j0 + cchunk)
        sh = xpad_ref[:, cs]
        acc = cb_ref[:, cs] + sh[halo:] * cw_ref[SSM_CONV - 1:SSM_CONV, cs]
        for k in range(SSM_CONV - 2, -1, -1):
            sh = pltpu.roll(sh, 1, axis=0)
            acc = acc + sh[halo:] * cw_ref[k:k + 1, cs]
        act_ref[slot, :, cs] = _silu(acc)

    def tail_of(ref):
        return ref[n_valid - (SSM_CONV - 1):n_valid, :]

    @pl.when(c == 0)
    def _():
        halo_ref[...] = jnp.zeros(halo_ref.shape, F32)
        halo_ref[tail0:halo, :] = conv0_ref[0]
        xpad_ref[0:halo, :] = halo_ref[...]
        if rows_in < q:
            xpad_ref[halo + rows_in:halo + q, :] = jnp.zeros((q - rows_in, SSM_CONV_DIM), F32)
        xpad_ref[halo:halo + rows_in, :] = xbc_ref[...]
        for j0 in range(0, SSM_CONV_DIM, cchunk):
            conv_cols(0, j0)
        halo_ref[tail0:halo, :] = tail_of(xbc_ref)
        for g in range(SSM_GROUPS):
            ht_ref[g] = h0_ref[0, g * SSM_HPG:(g + 1) * SSM_HPG].reshape(gw, SSM_STATE).T

    @pl.when(c == last_c)
    def _():
        convout_ref[0] = halo_ref[tail0:halo, :]

    if lookahead:
        xpad_ref[0:halo, :] = halo_ref[...]
        xpad_ref[halo:halo + rows_in, :] = xnx_ref[...]

    row = lax.broadcasted_iota(jnp.int32, (q, LANES), 0)
    dt_raw = dt_ref[...]
    if rows_in < q:
        dt_raw = jnp.concatenate([dt_raw, jnp.zeros((q - rows_in, LANES), F32)], axis=0)
    dt = _softplus(dt_raw + dtb_ref[...])
    if n_valid < q:
        dt = jnp.where(row < n_valid, dt, 0.0)
    adt = dt * (-jnp.exp(alog_ref[...]))
    acum = adt
    k = 1
    while k < q:
        acum = acum + jnp.where(row >= k, pltpu.roll(acum, k, axis=0), 0.0)
        k *= 2
    act_t_ref[...] = acum.T
    eac = jnp.exp(acum)
    dec = jnp.exp(acum[q - 1:q, :] - acum)
    for i, val in enumerate((dt, dt * dec, eac)):
        hi, mid = _pieces2(val)
        ex = jnp.dot(jnp.concatenate([hi, mid], axis=0), e_ref[...], preferred_element_type=F32)
        ex_ref[i] = ex[:q] + ex[q:]

    r_i = lax.broadcasted_iota(jnp.int32, (q, q), 0)
    c_i = lax.broadcasted_iota(jnp.int32, (q, q), 1)
    causal = r_i >= c_i
    blk_r = lax.broadcasted_iota(jnp.int32, (SSM_HPG * q, gw), 0) // q
    blk_c = lax.broadcasted_iota(jnp.int32, (SSM_HPG * q, gw), 1) // SSM_HEAD_DIM
    diag_blocks = blk_r == blk_c

    gn = SSM_GROUPS * SSM_STATE
    for g in range(SSM_GROUPS):
        gs = slice(g * gw, (g + 1) * gw)
        if lookahead:
            conv_cols(1 - cur, g * cchunk)
        bt = act_ref[cur, :, SSM_D_INNER + g * SSM_STATE:SSM_D_INNER + (g + 1) * SSM_STATE].T.astype(BF16)
        c_g = act_ref[cur, :, SSM_D_INNER + gn + g * SSM_STATE:SSM_D_INNER + gn + (g + 1) * SSM_STATE].astype(BF16)
        cb = jnp.dot(c_g, bt, preferred_element_type=F32)
        mats = []
        for r in range(SSM_HPG):
            h = g * SSM_HPG + r
            seg = acum[:, h:h + 1] - act_t_ref[h:h + 1, :]
            mats.append((cb * jnp.exp(jnp.where(causal, seg, -jnp.inf))).astype(BF16))
        mcat = jnp.concatenate(mats, axis=1)
        xs_g = act_ref[cur, :, gs]
        xdt = (xs_g * ex_ref[0, :, gs]).astype(BF16)
        bd = jnp.where(diag_blocks, jnp.concatenate([xdt] * SSM_HPG, axis=0), jnp.zeros((), BF16))
        y_diag = jnp.dot(mcat, bd, preferred_element_type=F32)
        ht = ht_ref[g]
        y_off = jnp.dot(c_g, ht.astype(BF16), preferred_element_type=F32)
        eac_g = ex_ref[2, :, gs]
        y = y_diag + y_off * eac_g + xs_g * dexp_ref[:, gs]
        y_ref[:, gs] = y[:rows_in]
        xdtd = (xs_g * ex_ref[1, :, gs]).astype(BF16)
        ht_ref[g] = ht * eac_g[q - 1:q, :] + jnp.dot(bt, xdtd, preferred_element_type=F32)

    if lookahead:
        @pl.when(c < last_c)
        def _():
            halo_ref[tail0:halo, :] = tail_of(xnx_ref)

    @pl.when(c == last_c)
    def _():
        for g in range(SSM_GROUPS):
            hout_ref[0, g * SSM_HPG:(g + 1) * SSM_HPG] = ht_ref[g].T.reshape(SSM_HPG, SSM_HEAD_DIM, SSM_STATE)


def _ssd(om, od, cw, cb, dtb, alog, dexp, emat, conv0, h0, nb, nc, rows_in, n_valid):
    m = om.shape[0]
    gw = SSM_HPG * SSM_HEAD_DIM
    kern = functools.partial(_ssd_kernel, rows_in=rows_in, n_valid=n_valid, lookahead=nc > 1)
    vec = lambda: pl.BlockSpec((1, LANES), lambda b, c: (0, 0))
    return pl.pallas_call(
        kern,
        grid=(nb, nc),
        in_specs=[
            pl.BlockSpec((rows_in, SSM_CONV_DIM), lambda b, c: (b * nc + c, 0)),
            pl.BlockSpec((rows_in, SSM_CONV_DIM), lambda b, c: (b * nc + jnp.minimum(c + 1, nc - 1), 0)),
            pl.BlockSpec((rows_in, LANES), lambda b, c: (b * nc + c, 0)),
            pl.BlockSpec((SSM_CONV, SSM_CONV_DIM), lambda b, c: (0, 0)),
            pl.BlockSpec((1, SSM_CONV_DIM), lambda b, c: (0, 0)),
            vec(), vec(),
            pl.BlockSpec((1, SSM_D_INNER), lambda b, c: (0, 0)),
            pl.BlockSpec((LANES, SSM_D_INNER), lambda b, c: (0, 0)),
            pl.BlockSpec((1, SSM_CONV - 1, SSM_CONV_DIM), lambda b, c: (b, 0, 0)),
            pl.BlockSpec((1, SSM_HEADS, SSM_HEAD_DIM, SSM_STATE), lambda b, c: (b, 0, 0, 0)),
        ],
        out_specs=[
            pl.BlockSpec((rows_in, SSM_D_INNER), lambda b, c: (b * nc + c, 0)),
            pl.BlockSpec((1, SSM_HEADS, SSM_HEAD_DIM, SSM_STATE), lambda b, c: (b, 0, 0, 0)),
            pl.BlockSpec((1, SSM_CONV - 1, SSM_CONV_DIM), lambda b, c: (b, 0, 0)),
        ],
        out_shape=[
            jax.ShapeDtypeStruct((m, SSM_D_INNER), F32),
            jax.ShapeDtypeStruct((nb, SSM_HEADS, SSM_HEAD_DIM, SSM_STATE), F32),
            jax.ShapeDtypeStruct((nb, SSM_CONV - 1, SSM_CONV_DIM), F32),
        ],
        scratch_shapes=[
            pltpu.VMEM((SUBLANES + SSM_CHUNK, SSM_CONV_DIM), F32),
            pltpu.VMEM((SUBLANES, SSM_CONV_DIM), F32),
            pltpu.VMEM((2, SSM_CHUNK, SSM_CONV_DIM), F32),
            pltpu.VMEM((SSM_GROUPS, SSM_STATE, gw), F32),
            pltpu.VMEM((LANES, SSM_CHUNK), F32),
            pltpu.VMEM((3, SSM_CHUNK, SSM_D_INNER), F32),
        ],
        compiler_params=_cparams(("arbitrary", "arbitrary")),
        name="ssd",
    )(om, om, od, cw, cb, dtb, alog, dexp, emat, conv0, h0)


def _outproj0_kernel(y_ref, z_ref, nw_ref, w_ref, np_ref, x_ref, o_ref):
    h = y_ref[...] * _silu(z_ref[...])
    gw = SSM_D_INNER // SSM_GROUPS
    hn = jnp.concatenate([_rms(h[:, g * gw:(g + 1) * gw]) for g in range(SSM_GROUPS)], axis=1)
    hn = (hn * nw_ref[...]).astype(BF16)
    o = jnp.dot(hn, w_ref[...], preferred_element_type=F32)
    o_ref[...] = x_ref[...] + _rms(o) * np_ref[...]


def _outproj0(y, om, nw, w, npost, x, tm):
    m = y.shape[0]
    zblk = SSM_CONV_DIM // SSM_D_INNER
    return pl.pallas_call(
        _outproj0_kernel,
        grid=(m // tm,),
        in_specs=[
            pl.BlockSpec((tm, SSM_D_INNER), lambda i: (i, 0)),
            pl.BlockSpec((tm, SSM_D_INNER), lambda i: (i, zblk)),
            pl.BlockSpec((1, SSM_D_INNER), lambda i: (0, 0)),
            pl.BlockSpec((SSM_D_INNER, D_MODEL), lambda i: (0, 0)),
            pl.BlockSpec((1, D_MODEL), lambda i: (0, 0)),
            pl.BlockSpec((tm, D_MODEL), lambda i: (i, 0)),
        ],
        out_specs=pl.BlockSpec((tm, D_MODEL), lambda i: (i, 0)),
        out_shape=jax.ShapeDtypeStruct((m, D_MODEL), F32),
        compiler_params=_cparams(("parallel",)),
        name="outproj0",
    )(y, om, nw, w, npost, x)


def _inproj1p_kernel(x_ref, g_ref, wt_ref, wft_ref, bf_ref, qt_ref, kt_ref, vt_ref, vtb_ref, zt_ref, lft_ref):
    un = (_rms(x_ref[...]) * g_ref[...]).astype(BF16)
    w = ATTN_WIDTH
    proj = lambda i: lax.dot_general(wt_ref[i * w:(i + 1) * w, :], un, _NT, preferred_element_type=F32)
    qt_ref[0] = (proj(0) * (ATTN_HEAD_DIM ** -0.5 * LOG2E)).astype(BF16)
    kt_ref[0] = proj(1)
    vt = proj(2)
    vt_ref[0] = vt
    vtb_ref[0] = vt.astype(BF16)
    zt_ref[0] = proj(3)
    ft = lax.dot_general(wft_ref[...], un, _NT, preferred_element_type=F32)
    lft_ref[0] = _log_sigmoid(ft + bf_ref[...])


def _inproj1p(x, g, wt, wft, bfc, nb, seq, tm):
    m, d = x.shape
    nq = seq // tm
    w = ATTN_WIDTH
    full = lambda r, c: pl.BlockSpec((r, c), lambda b, i: (0, 0))
    feat = lambda r: pl.BlockSpec((1, r, tm), lambda b, i: (b, 0, i))
    fshape = lambda dt: jax.ShapeDtypeStruct((nb, w, seq), dt)
    return pl.pallas_call(
        _inproj1p_kernel,
        grid=(nb, nq),
        in_specs=[pl.BlockSpec((tm, d), lambda b, i: (b * nq + i, 0)), full(1, d), full(4 * w, d),
                  full(ATTN_HEADS, d), full(ATTN_HEADS, 1)],
        out_specs=[feat(w), feat(w), feat(w), feat(w), feat(w), feat(ATTN_HEADS)],
        out_shape=[fshape(BF16), fshape(F32), fshape(F32), fshape(BF16), fshape(F32),
                   jax.ShapeDtypeStruct((nb, ATTN_HEADS, seq), F32)],
        compiler_params=_cparams(("parallel", "parallel")),
        name="inproj1p",
    )(x, g, wt, wft, bfc)


def _inproj1s_kernel(x_ref, g_ref, wt_ref, wf_ref, bf_ref, o_ref, lf_ref):
    un = (_rms(x_ref[...]) * g_ref[...]).astype(BF16)
    o_ref[...] = lax.dot_general(un, wt_ref[...], _NT, preferred_element_type=F32)
    lf_ref[...] = _log_sigmoid(jnp.dot(un, wf_ref[...], preferred_element_type=F32) + bf_ref[...])


def _inproj1s(x, g, wt, wf, bfr):
    m, d = x.shape
    n = wt.shape[0]
    return pl.pallas_call(
        _inproj1s_kernel,
        grid=(1,),
        in_specs=[pl.BlockSpec((m, d), lambda i: (0, 0)), pl.BlockSpec((1, d), lambda i: (0, 0)),
                  pl.BlockSpec((n, d), lambda i: (0, 0)), pl.BlockSpec((d, LANES), lambda i: (0, 0)),
                  pl.BlockSpec((1, LANES), lambda i: (0, 0))],
        out_specs=[pl.BlockSpec((m, n), lambda i: (0, 0)), pl.BlockSpec((m, LANES), lambda i: (0, 0))],
        out_shape=[jax.ShapeDtypeStruct((m, n), F32), jax.ShapeDtypeStruct((m, LANES), F32)],
        compiler_params=_cparams(("arbitrary",)),
        name="inproj1s",
    )(x, g, wt, wf, bfr)


def _cumsum_kernel(x_ref, o_ref):
    nblk = x_ref.shape[2] // LANES
    upper = _tri(LANES, lambda r, c: r <= c)

    def body(j, carry):
        off = pl.multiple_of(j * LANES, LANES)
        loc = _dot3(x_ref[0, :, pl.ds(off, LANES)], upper)
        o_ref[0, :, pl.ds(off, LANES)] = loc + carry
        return carry + loc[:, LANES - 1:LANES]

    lax.fori_loop(0, nblk, body, jnp.zeros((x_ref.shape[1], 1), F32))


def _cumsum(x):
    nb, h, seq = x.shape
    return pl.pallas_call(
        _cumsum_kernel,
        grid=(nb,),
        in_specs=[pl.BlockSpec((1, h, seq), lambda b: (b, 0, 0))],
        out_specs=pl.BlockSpec((1, h, seq), lambda b: (b, 0, 0)),
        out_shape=jax.ShapeDtypeStruct((nb, h, seq), F32),
        compiler_params=_cparams(("parallel",)),
        name="lf_cumsum",
    )(x)


def _bias_rows(c2, first, n, width):
    hi, mid, lo = (p.astype(F32) for p in _split3(c2))
    row = lax.broadcasted_iota(jnp.int32, (n, width), 0)
    ones_first = 3 - first
    out = jnp.where((row >= ones_first) & (row < ones_first + 3), 1.0, 0.0)
    for i, piece in enumerate((hi, mid, lo)):
        out = jnp.where(row == first + i, piece, out)
    return out


def _kaug_kernel(kt_ref, c_ref, o_ref, *, chunk):
    h = pl.program_id(1)

    def body(j, carry):
        off = pl.multiple_of(j * chunk, chunk)
        c2 = c_ref[0, pl.ds(h, 1), pl.ds(off, chunk)] * LOG2E
        full = jnp.concatenate([kt_ref[0, :, pl.ds(off, chunk)],
                                _bias_rows(-c2, 3, ATTN_HEAD_DIM, chunk)], axis=0)
        o_ref[0, 0, pl.ds(off, chunk), :] = full.T.astype(BF16)
        return carry

    lax.fori_loop(0, kt_ref.shape[2] // chunk, body, 0)


def _kaug(kt, c, chunk):
    nb, w, seq = kt.shape
    return pl.pallas_call(
        functools.partial(_kaug_kernel, chunk=chunk),
        grid=(nb, ATTN_HEADS),
        in_specs=[pl.BlockSpec((1, ATTN_HEAD_DIM, seq), lambda b, h: (b, h, 0)),
                  pl.BlockSpec((1, ATTN_HEADS, seq), lambda b, h: (b, 0, 0))],
        out_specs=pl.BlockSpec((1, 1, seq, LANES), lambda b, h: (b, h, 0, 0)),
        out_shape=jax.ShapeDtypeStruct((nb, ATTN_HEADS, seq, LANES), BF16),
        compiler_params=_cparams(("parallel", "parallel")),
        name="fox_kaug",
    )(kt, c)


def _flash_body(qi, kj, qt_ref, ka_ref, vt_ref, cq_ref, o_ref, qa_ref, m_ref, acc_ref, s_ref, *, tq, tk, between):
    hd = ATTN_HEAD_DIM

    @pl.when(kj == 0)
    def _():
        m_ref[...] = jnp.full(m_ref.shape, NEG_BIG, F32)
        acc_ref[...] = jnp.zeros(acc_ref.shape, F32)

        def build(h, carry):
            off = pl.multiple_of(h * hd, hd)
            c2 = cq_ref[0, pl.ds(h, 1), :] * LOG2E
            qa_ref[h, 0:hd, :] = qt_ref[0, pl.ds(off, hd), :]
            qa_ref[h, hd:2 * hd, :] = _bias_rows(c2, 0, hd, tq).astype(BF16)
            return carry

        lax.fori_loop(0, ATTN_HEADS, build, 0)

    def step(masked):
        if masked:
            key = kj * tk + lax.broadcasted_iota(jnp.int32, (tk, tq), 0)
            qry = qi * tq + lax.broadcasted_iota(jnp.int32, (tk, tq), 1)
            keep = key <= qry
        ones = jnp.ones((AUG, tk), BF16)

        def scores(h, slot):
            st = jnp.dot(ka_ref[0, h], qa_ref[h], preferred_element_type=F32)
            if masked:
                st = jnp.where(keep, st, NEG_BIG)
            s_ref[slot] = st
            return jnp.max(st, axis=0, keepdims=True)

        def softmax_pv(h, slot, mx):
            off = pl.multiple_of(h * hd, hd)
            m_prev = m_ref[h]
            m_new = jnp.maximum(m_prev, mx)
            pt = jnp.exp2(s_ref[slot] - m_new[0:1]).astype(BF16)
            va = jnp.concatenate([vt_ref[0, pl.ds(off, hd), :], ones], axis=0)
            pv = jnp.dot(va, pt, preferred_element_type=F32)
            acc_ref[h] = acc_ref[h] * jnp.exp2(m_prev - m_new)[0:1] + pv
            m_ref[h] = m_new

        mx = scores(0, 0)
        for h in range(ATTN_HEADS):
            nxt = scores(h + 1, (h + 1) % 2) if h + 1 < ATTN_HEADS else None
            between(h)
            softmax_pv(h, h % 2, mx)
            mx = nxt

    @pl.when(kj < qi)
    def _():
        step(False)

    @pl.when(kj == qi)
    def _():
        step(True)

        def fin(h, carry):
            off = pl.multiple_of(h * hd, hd)
            a = acc_ref[h]
            o_ref[0, pl.ds(off, hd), :] = a[0:hd] / a[hd:hd + 1]
            return carry

        lax.fori_loop(0, ATTN_HEADS, fin, 0)


class _Decode:
    def __init__(self, q_ref, kn_ref, vn_ref, lfn_ref, k_refs, v_refs, lf_refs,
                 o_ref, qg_ref, m_ref, l_ref, acc_ref, carry_ref, lq):
        self.__dict__.update(locals())
        self.hd = ATTN_HEAD_DIM
        self.ngrp = DECODE_HEAD_GROUPS
        self.hpg = ATTN_HEADS // self.ngrp
        self.grow = self.hpg * lq
        self.gcol = self.hpg * self.hd
        self.nrow = ATTN_HEADS * lq

    def rsl(self, g):
        return slice(g * self.grow, (g + 1) * self.grow)

    def csl(self, g):
        return slice(g * self.gcol, (g + 1) * self.gcol)

    def init(self):
        lq, hd = self.lq, self.hd
        rh = lax.broadcasted_iota(jnp.int32, (self.grow, self.gcol), 0) // lq
        ch = lax.broadcasted_iota(jnp.int32, (self.grow, self.gcol), 1) // hd
        for g in range(self.ngrp):
            qt = jnp.concatenate([self.q_ref[0, :, self.csl(g)] * (hd ** -0.5)] * self.hpg, axis=0)
            self.qg_ref[g] = jnp.where(rh == ch, qt, 0.0).astype(BF16)
        self.m_ref[...] = jnp.full(self.m_ref.shape, NEG_BIG, F32)
        self.l_ref[...] = jnp.zeros(self.l_ref.shape, F32)
        self.acc_ref[...] = jnp.zeros(self.acc_ref.shape, F32)
        self.carry_ref[...] = jnp.zeros(self.carry_ref.shape, F32)

    def expand_heads(self, r):
        return jnp.concatenate(
            [jnp.broadcast_to(r[h:h + 1, :], (self.lq, r.shape[1])) for h in range(ATTN_HEADS)], axis=0)

    def merge(self, s_list, pv_fn):
        m_prev = self.m_ref[...]
        m_new = m_prev
        for s in s_list:
            m_new = jnp.maximum(m_new, jnp.max(s, axis=1, keepdims=True))
        alpha = jnp.exp(m_prev - m_new)
        lsum = jnp.zeros_like(m_prev)
        pv = [None] * self.ngrp
        for i, s in enumerate(s_list):
            p = jnp.exp(s - m_new)
            lsum = lsum + jnp.sum(p, axis=1, keepdims=True)
            pb = p.astype(BF16)
            for g in range(self.ngrp):
                d = pv_fn(i, g, pb[self.rsl(g)])
                pv[g] = d if pv[g] is None else pv[g] + d
        self.l_ref[...] = alpha * self.l_ref[...] + lsum
        for g in range(self.ngrp):
            self.acc_ref[g] = alpha[self.rsl(g)] * self.acc_ref[g] + pv[g]
        self.m_ref[...] = m_new

    def pages(self, plist, active):
        later = _tri(PAGE_SIZE, lambda r, c: r > c)
        carry = self.carry_ref[...]
        s_list, v_list = [], []
        for p in plist:
            lf = jnp.where(active, self.lf_refs[p][0], 0.0)
            r = _dot3(lf, later) + carry
            carry = carry + jnp.sum(lf, axis=1, keepdims=True)
            kt = self.k_refs[p][0].astype(BF16)
            s = jnp.concatenate([jnp.dot(self.qg_ref[g], kt[self.csl(g)], preferred_element_type=F32)
                                 for g in range(self.ngrp)], axis=0)
            s_list.append(jnp.where(active, s + self.expand_heads(r), NEG_BIG))
            v_list.append(self.v_refs[p][0].astype(BF16))
        self.carry_ref[...] = carry
        self.merge(s_list, lambda i, g, pr: lax.dot_general(
            pr, v_list[i][self.csl(g)], _NT, preferred_element_type=F32))

    def finish(self):
        lq, hd = self.lq, self.hd
        pad = jnp.zeros((LANES - lq, ATTN_WIDTH), F32)
        kn = jnp.concatenate([self.kn_ref[0], pad], axis=0).astype(BF16)
        vn = jnp.concatenate([self.vn_ref[0], pad], axis=0).astype(BF16)
        s = jnp.concatenate([lax.dot_general(self.qg_ref[g], kn[:, self.csl(g)], _NT, preferred_element_type=F32)
                             for g in range(self.ngrp)], axis=0)
        cn = _dot3(self.lfn_ref[0], _tri(LANES, lambda r, c: r <= c))
        s = s - self.expand_heads(cn)
        qpos = lax.broadcasted_iota(jnp.int32, (self.nrow, LANES), 0) % lq
        kpos = lax.broadcasted_iota(jnp.int32, (self.nrow, LANES), 1)
        s = jnp.where(kpos <= qpos, s, NEG_BIG)
        self.merge([s], lambda i, g, pr: jnp.dot(pr, vn[:, self.csl(g)], preferred_element_type=F32))
        for g in range(self.ngrp):
            out = self.acc_ref[g] / self.l_ref[self.rsl(g), :]
            for hl in range(self.hpg):
                h = g * self.hpg + hl
                self.o_ref[0, :, h * hd:(h + 1) * hd] = out[hl * lq:(hl + 1) * lq, hl * hd:(hl + 1) * hd]


def _attn_kernel(bt_ref, qit_ref, kjt_ref, pt_ref, qt_ref, ka_ref, vt_ref, cq_ref,
                 qs_ref, kn_ref, vn_ref, lfn_ref, *refs, tq, tk, pps, lq, n_dec, spe):
    del bt_ref, pt_ref
    k_refs = refs[0:pps]
    v_refs = refs[pps:2 * pps]
    lf_refs = refs[2 * pps:3 * pps]
    (o_ref, os_ref, qa_ref, m_ref, acc_ref, s_ref, qg_ref, md_ref, ld_ref, accd_ref, carry_ref) = refs[3 * pps:]
    t = pl.program_id(0)
    active = t < n_dec
    js = t % spe
    dec = _Decode(qs_ref, kn_ref, vn_ref, lfn_ref, k_refs, v_refs, lf_refs,
                  os_ref, qg_ref, md_ref, ld_ref, accd_ref, carry_ref, lq)

    @pl.when(active & (js == 0))
    def _():
        dec.init()

    nbatch = pl.cdiv(pps, DECODE_PAGES_PER_MERGE)
    stride = ATTN_HEADS // nbatch

    def pages_between(h):
        if h % stride == 0 and h // stride < nbatch:
            b0 = (h // stride) * DECODE_PAGES_PER_MERGE
            dec.pages(range(b0, min(b0 + DECODE_PAGES_PER_MERGE, pps)), active)

    _flash_body(qit_ref[t], kjt_ref[t], qt_ref, ka_ref, vt_ref, cq_ref, o_ref, qa_ref, m_ref, acc_ref, s_ref,
                tq=tq, tk=tk, between=pages_between)

    @pl.when(active & (js == spe - 1))
    def _():
        dec.finish()


def _attention(qt, ka, vtb, c, q_s, kn, vn, lfn_t, ck, cv, clf, page_table, tq, pps):
    nb, w, seq = qt.shape
    nbs, lq, _ = q_s.shape
    npages = page_table.shape[1]
    nq = seq // tq
    tri = [(b, i, j) for b in range(nb) for i in range(nq) for j in range(i + 1)]
    nsteps = len(tri)
    spe = npages // pps
    n_dec = nbs * spe
    assert n_dec <= nsteps and npages % pps == 0, (n_dec, nsteps)
    bt, qit, kjt = (jnp.asarray([x[k] for x in tri], jnp.int32) for k in range(3))
    nrow = ATTN_HEADS * lq
    ng = DECODE_HEAD_GROUPS
    kern = functools.partial(_attn_kernel, tq=tq, tk=tq, pps=pps, lq=lq, n_dec=n_dec, spe=spe)

    def dec_elem(t):
        return jnp.minimum(t // spe, nbs - 1)

    def page_spec(shape, p):
        def imap(t, bt, qit, kjt, pt):
            td = jnp.minimum(t, n_dec - 1)
            return (pt[td // spe, npages - 1 - ((td % spe) * pps + p)], 0, 0)
        return pl.BlockSpec(shape, imap)

    tokspec = lambda: pl.BlockSpec((1, lq, w), lambda t, bt, qit, kjt, pt: (dec_elem(t), 0, 0))
    in_specs = [
        pl.BlockSpec((1, w, tq), lambda t, bt, qit, kjt, pt: (bt[t], 0, qit[t])),
        pl.BlockSpec((1, ATTN_HEADS, tq, LANES), lambda t, bt, qit, kjt, pt: (bt[t], 0, kjt[t], 0)),
        pl.BlockSpec((1, w, tq), lambda t, bt, qit, kjt, pt: (bt[t], 0, kjt[t])),
        pl.BlockSpec((1, ATTN_HEADS, tq), lambda t, bt, qit, kjt, pt: (bt[t], 0, qit[t])),
        tokspec(), tokspec(), tokspec(),
        pl.BlockSpec((1, ATTN_HEADS, LANES), lambda t, bt, qit, kjt, pt: (dec_elem(t), 0, 0)),
    ]
    in_specs += [page_spec((1, w, PAGE_SIZE), p) for p in range(pps)]
    in_specs += [page_spec((1, w, PAGE_SIZE), p) for p in range(pps)]
    in_specs += [page_spec((1, ATTN_HEADS, PAGE_SIZE), p) for p in range(pps)]
    return pl.pallas_call(
        kern,
        grid_spec=pltpu.PrefetchScalarGridSpec(
            num_scalar_prefetch=4,
            grid=(nsteps,),
            in_specs=in_specs,
            out_specs=[
                pl.BlockSpec((1, w, tq), lambda t, bt, qit, kjt, pt: (bt[t], 0, qit[t])),
                pl.BlockSpec((1, lq, w), lambda t, bt, qit, kjt, pt: (dec_elem(t), 0, 0)),
            ],
            scratch_shapes=[
                pltpu.VMEM((ATTN_HEADS, 2 * ATTN_HEAD_DIM, tq), BF16),
                pltpu.VMEM((ATTN_HEADS, SUBLANES, tq), F32),
                pltpu.VMEM((ATTN_HEADS, ATTN_HEAD_DIM + AUG, tq), F32),
                pltpu.VMEM((2, tq, tq), F32),
                pltpu.VMEM((ng, nrow // ng, w // ng), BF16),
                pltpu.VMEM((nrow, 1), F32),
                pltpu.VMEM((nrow, 1), F32),
                pltpu.VMEM((ng, nrow // ng, w // ng), F32),
                pltpu.VMEM((ATTN_HEADS, 1), F32),
            ],
        ),
        out_shape=[jax.ShapeDtypeStruct((nb, w, seq), F32), jax.ShapeDtypeStruct((nbs, lq, w), F32)],
        compiler_params=pltpu.CompilerParams(dimension_semantics=("arbitrary",), vmem_limit_bytes=ATTN_VMEM_LIMIT),
        name="fox_attention",
    )(bt, qit, kjt, page_table, qt, ka, vtb, c, q_s, kn, vn, lfn_t, *([ck] * pps), *([cv] * pps), *([clf] * pps))


def _outproj1_kernel(o_ref, z_ref, w_ref, np_ref, x_ref, out_ref):
    g = (o_ref[...] * _silu(z_ref[...])).astype(BF16)
    y = jnp.dot(g, w_ref[...], preferred_element_type=F32)
    out_ref[...] = x_ref[...] + _rms(y) * np_ref[...]


def _outproj1(o, z, zblk, w, npost, x, tm):
    m = o.shape[0]
    return pl.pallas_call(
        _outproj1_kernel,
        grid=(m // tm,),
        in_specs=[
            pl.BlockSpec((tm, ATTN_WIDTH), lambda i: (i, 0)),
            pl.BlockSpec((tm, ATTN_WIDTH), lambda i: (i, zblk)),
            pl.BlockSpec((ATTN_WIDTH, D_MODEL), lambda i: (0, 0)),
            pl.BlockSpec((1, D_MODEL), lambda i: (0, 0)),
            pl.BlockSpec((tm, D_MODEL), lambda i: (i, 0)),
        ],
        out_specs=pl.BlockSpec((tm, D_MODEL), lambda i: (i, 0)),
        out_shape=jax.ShapeDtypeStruct((m, D_MODEL), F32),
        compiler_params=_cparams(("parallel",)),
        name="outproj1",
    )(o, z, w, npost, x)


def _outproj1p_kernel(ot_ref, zt_ref, w_ref, np_ref, x_ref, out_ref):
    gt = (ot_ref[0] * _silu(zt_ref[0])).astype(BF16)
    y = lax.dot_general(gt, w_ref[...], _TN, preferred_element_type=F32)
    out_ref[...] = x_ref[...] + _rms(y) * np_ref[...]


def _outproj1p(ot, zt, w, npost, x, tm):
    nb, wd, seq = ot.shape
    nq = seq // tm
    feat = lambda: pl.BlockSpec((1, wd, tm), lambda b, i: (b, 0, i))
    return pl.pallas_call(
        _outproj1p_kernel,
        grid=(nb, nq),
        in_specs=[feat(), feat(),
                  pl.BlockSpec((wd, D_MODEL), lambda b, i: (0, 0)),
                  pl.BlockSpec((1, D_MODEL), lambda b, i: (0, 0)),
                  pl.BlockSpec((tm, D_MODEL), lambda b, i: (b * nq + i, 0))],
        out_specs=pl.BlockSpec((tm, D_MODEL), lambda b, i: (b * nq + i, 0)),
        out_shape=jax.ShapeDtypeStruct((nb * seq, D_MODEL), F32),
        compiler_params=_cparams(("parallel", "parallel")),
        name="outproj1p",
    )(ot, zt, w, npost, x)


def _pad_lanes(v):
    return jnp.pad(v.astype(F32), (0, LANES - v.shape[0])).reshape(1, LANES)


def kernel(x_prompt, x_sample, state_ssm, state_conv, cache_k, cache_v, cache_logf, page_table, norm_pre, norm_post, ssm_w_in, ssm_conv_w, ssm_conv_b, ssm_dt_bias, ssm_a_log, ssm_d, ssm_norm_w, ssm_w_out, attn_w_in, attn_b_f, attn_w_out):
    nbp, seq, d = x_prompt.shape
    nbs, lq, _ = x_sample.shape
    xp = x_prompt.reshape(nbp * seq, d)
    xs = x_sample.reshape(nbs * lq, d)
    tm_p = min(1024, seq)

    w_in = ssm_w_in[0]
    z_end = SSM_D_INNER
    xbc_end = SSM_D_INNER + SSM_CONV_DIM
    wm = jnp.concatenate([w_in[:, z_end:xbc_end], w_in[:, :z_end]], axis=1).astype(BF16)
    wd = jnp.pad(w_in[:, xbc_end:], ((0, 0), (0, LANES - SSM_HEADS))).astype(BF16)
    g0 = norm_pre[0:1]
    head_of_lane = jnp.arange(SSM_D_INNER, dtype=jnp.int32) // SSM_HEAD_DIM
    expand = (jnp.arange(LANES, dtype=jnp.int32)[:, None] == head_of_lane[None, :]).astype(BF16)
    d_lanes = jnp.repeat(ssm_d[0].astype(F32), SSM_HEAD_DIM).reshape(1, SSM_D_INNER)
    ssd_vecs = (ssm_conv_w[0], ssm_conv_b[0:1], _pad_lanes(ssm_dt_bias[0]), _pad_lanes(ssm_a_log[0]),
                d_lanes, expand)
    w_out0 = ssm_w_out[0].astype(BF16)
    nw0 = ssm_norm_w[0:1]
    np0 = norm_post[0:1]

    om_p, od_p = _inproj0(xp, g0, wm, wd, tm_p)
    om_s, od_s = _inproj0(xs, g0, wm, wd, nbs * lq)

    conv0_p = jnp.zeros((nbp, SSM_CONV - 1, SSM_CONV_DIM), F32)
    h0_p = jnp.zeros((nbp, SSM_HEADS, SSM_HEAD_DIM, SSM_STATE), F32)
    y_p, ssm_p, conv_p = _ssd(om_p, od_p, *ssd_vecs, conv0_p, h0_p, nbp, seq // SSM_CHUNK, SSM_CHUNK, SSM_CHUNK)
    y_s, ssm_s, conv_s = _ssd(om_s, od_s, *ssd_vecs, state_conv[0], state_ssm[0], nbs, 1, lq, lq)

    hp1 = _outproj0(y_p, om_p, nw0, w_out0, np0, xp, min(512, seq))
    hs1 = _outproj0(y_s, om_s, nw0, w_out0, np0, xs, nbs * lq)

    wa = attn_w_in[0]
    w = ATTN_WIDTH
    wa_t = wa.T.astype(BF16)
    wf = jnp.pad(wa[:, 4 * w:], ((0, 0), (0, LANES - ATTN_HEADS))).astype(BF16)
    g1 = norm_pre[1:2]
    np1 = norm_post[1:2]
    w_out1 = attn_w_out[0].astype(BF16)

    qt, kt, vt, vtb, zt, lft = _inproj1p(
        hp1, g1, wa_t[:4 * w], wa_t[4 * w:], attn_b_f[0].astype(F32).reshape(ATTN_HEADS, 1),
        nbp, seq, min(512, seq))
    c_p = _cumsum(lft)
    ka = _kaug(kt, c_p, min(1024, seq))
    tq = min(512, seq)

    proj_s, lf_s = _inproj1s(hs1, g1, wa_t[:4 * w], wf, _pad_lanes(attn_b_f[0]))
    q_s = proj_s[:, :w].reshape(nbs, lq, w)
    k_s = proj_s[:, w:2 * w].reshape(nbs, lq, w)
    v_s = proj_s[:, 2 * w:3 * w].reshape(nbs, lq, w)
    lf_s = lf_s[:, :ATTN_HEADS].reshape(nbs, lq, ATTN_HEADS)
    lfn_t = jnp.pad(lf_s.transpose(0, 2, 1), ((0, 0), (0, 0), (0, LANES - lq)))
    npool = cache_k.shape[1]
    ck = cache_k[0].transpose(0, 2, 3, 1).reshape(npool, w, PAGE_SIZE)
    cv = cache_v[0].transpose(0, 2, 3, 1).reshape(npool, w, PAGE_SIZE)
    clf = cache_logf[0].transpose(0, 2, 1)
    ot, o_s = _attention(qt, ka, vtb, c_p, q_s, k_s, v_s, lfn_t, ck, cv, clf, page_table, tq,
                         min(DECODE_PAGES_PER_STEP, page_table.shape[1]))
    y_prompt = _outproj1p(ot, zt, w_out1, np1, hp1, min(512, seq))
    y_sample = _outproj1(o_s.reshape(nbs * lq, w), proj_s, 3, w_out1, np1, hs1, nbs * lq)

    hd = ATTN_HEAD_DIM
    return (
        y_prompt.reshape(nbp, seq, d),
        y_sample.reshape(nbs, lq, d),
        kt.reshape(nbp, ATTN_HEADS, hd, seq).transpose(0, 3, 1, 2)[None],
        vt.reshape(nbp, ATTN_HEADS, hd, seq).transpose(0, 3, 1, 2)[None],
        lft.transpose(0, 2, 1)[None],
        k_s.reshape(1, nbs, lq, ATTN_HEADS, hd),
        v_s.reshape(1, nbs, lq, ATTN_HEADS, hd),
        lf_s[None],
        ssm_p[None],
        conv_p[None],
        ssm_s[None],
        conv_s[None],
    )
```

```python
import functools

import jax
import jax.numpy as jnp
from jax import lax
from jax.experimental import pallas as pl
from jax.experimental.pallas import tpu as pltpu

F32 = jnp.float32
BF16 = jnp.bfloat16

D_MODEL = 1024
SSM_D_INNER = 2048
SSM_HEAD_DIM = 64
SSM_HEADS = 32
SSM_GROUPS = 8
SSM_HPG = SSM_HEADS // SSM_GROUPS
SSM_STATE = 128
SSM_CONV = 4
SSM_CONV_DIM = SSM_D_INNER + 2 * SSM_GROUPS * SSM_STATE
SSM_CHUNK = 128
ATTN_HEADS = 16
ATTN_HEAD_DIM = 64
ATTN_WIDTH = ATTN_HEADS * ATTN_HEAD_DIM
PAGE_SIZE = 128
RMS_EPS = 1e-6
LANES = 128
SUBLANES = 8
NEG_BIG = -1e30
LOG2E = 1.4426950408889634
AUG = 16
DECODE_PAGES_PER_STEP = 16
DECODE_PAGES_PER_MERGE = 8
ATTN_VMEM_LIMIT = 60 * 1024 * 1024
DECODE_HEAD_GROUPS = 4
VMEM_LIMIT = 56 * 1024 * 1024

_NT = (((1,), (1,)), ((), ()))
_TN = (((0,), (0,)), ((), ()))


def _silu(x):
    return x / (1.0 + jnp.exp(-x))


def _softplus(x):
    return jnp.maximum(x, 0.0) + jnp.log1p(jnp.exp(-jnp.abs(x)))


def _log_sigmoid(x):
    return jnp.minimum(x, 0.0) - jnp.log1p(jnp.exp(-jnp.abs(x)))


def _rms(x):
    return x * lax.rsqrt(jnp.mean(x * x, axis=-1, keepdims=True) + RMS_EPS)


def _split3(x):
    hi = x.astype(BF16)
    r1 = x - hi.astype(F32)
    mid = r1.astype(BF16)
    lo = (r1 - mid.astype(F32)).astype(BF16)
    return hi, mid, lo


def _pieces2(x):
    hi = x.astype(BF16)
    return hi, (x - hi.astype(F32)).astype(BF16)


def _dot3(x, tri):
    n = x.shape[0]
    hi, mid, lo = _split3(x)
    r = jnp.dot(jnp.concatenate([hi, mid, lo], axis=0), tri, preferred_element_type=F32)
    return r[:n] + r[n:2 * n] + r[2 * n:]


def _tri(n, fn):
    r = lax.broadcasted_iota(jnp.int32, (n, n), 0)
    c = lax.broadcasted_iota(jnp.int32, (n, n), 1)
    return jnp.where(fn(r, c), 1.0, 0.0).astype(BF16)


def _cparams(sem, **kw):
    return pltpu.CompilerParams(dimension_semantics=sem, vmem_limit_bytes=VMEM_LIMIT, **kw)


def _inproj0_kernel(x_ref, g_ref, wm_ref, wd_ref, om_ref, od_ref, un_ref):
    @pl.when(pl.program_id(1) == 0)
    def _():
        un = (_rms(x_ref[...]) * g_ref[...]).astype(BF16)
        un_ref[...] = un
        od_ref[...] = jnp.dot(un, wd_ref[...], preferred_element_type=F32)

    om_ref[...] = jnp.dot(un_ref[...], wm_ref[...], preferred_element_type=F32)


def _inproj0(x, g, wm, wd, tm, tn=3072):
    m, d = x.shape
    n = wm.shape[1]
    return pl.pallas_call(
        _inproj0_kernel,
        grid=(m // tm, n // tn),
        in_specs=[
            pl.BlockSpec((tm, d), lambda i, j: (i, 0)),
            pl.BlockSpec((1, d), lambda i, j: (0, 0)),
            pl.BlockSpec((d, tn), lambda i, j: (0, j)),
            pl.BlockSpec((d, LANES), lambda i, j: (0, 0)),
        ],
        out_specs=[
            pl.BlockSpec((tm, tn), lambda i, j: (i, j)),
            pl.BlockSpec((tm, LANES), lambda i, j: (i, 0)),
        ],
        out_shape=[jax.ShapeDtypeStruct((m, n), F32), jax.ShapeDtypeStruct((m, LANES), F32)],
        scratch_shapes=[pltpu.VMEM((tm, d), BF16)],
        compiler_params=_cparams(("parallel", "arbitrary")),
        name="inproj0",
    )(x, g, wm, wd)


def _ssd_kernel(xbc_ref, xnx_ref, dt_ref, cw_ref, cb_ref, dtb_ref, alog_ref, dexp_ref, e_ref, conv0_ref, h0_ref,
                y_ref, hout_ref, convout_ref,
                xpad_ref, halo_ref, act_ref, ht_ref, act_t_ref, ex_ref, *, rows_in, n_valid, lookahead):
    q = SSM_CHUNK
    gw = SSM_HPG * SSM_HEAD_DIM
    c = pl.program_id(1)
    last_c = pl.num_programs(1) - 1
    halo = SUBLANES
    tail0 = halo - (SSM_CONV - 1)
    cchunk = SSM_CONV_DIM // SSM_GROUPS
    cur = c % 2

    def conv_cols(slot, j0):
        cs = slice(j0, j0 + cchunk)
        sh = xpad_ref[:, cs]
        acc = cb_ref[:, cs] + sh[halo:] * cw_ref[SSM_CONV - 1:SSM_CONV, cs]
        for k in range(SSM_CONV - 2, -1, -1):
            sh = pltpu.roll(sh, 1, axis=0)
            acc = acc + sh[halo:] * cw_ref[k:k + 1, cs]
        act_ref[slot, :, cs] = _silu(acc)

    def tail_of(ref):
        return ref[n_valid - (SSM_CONV - 1):n_valid, :]

    @pl.when(c == 0)
    def _():
        halo_ref[...] = jnp.zeros(halo_ref.shape, F32)
        halo_ref[tail0:halo, :] = conv0_ref[0]
        xpad_ref[0:halo, :] = halo_ref[...]
        if rows_in < q:
            xpad_ref[halo + rows_in:halo + q, :] = jnp.zeros((q - rows_in, SSM_CONV_DIM), F32)
        xpad_ref[halo:halo + rows_in, :] = xbc_ref[...]
        for j0 in range(0, SSM_CONV_DIM, cchunk):
            conv_cols(0, j0)
        halo_ref[tail0:halo, :] = tail_of(xbc_ref)
        for g in range(SSM_GROUPS):
            ht_ref[g] = h0_ref[0, g * SSM_HPG:(g + 1) * SSM_HPG].reshape(gw, SSM_STATE).T

    @pl.when(c == last_c)
    def _():
        convout_ref[0] = halo_ref[tail0:halo, :]

    if lookahead:
        xpad_ref[0:halo, :] = halo_ref[...]
        xpad_ref[halo:halo + rows_in, :] = xnx_ref[...]

    row = lax.broadcasted_iota(jnp.int32, (q, LANES), 0)
    dt_raw = dt_ref[...]
    if rows_in < q:
        dt_raw = jnp.concatenate([dt_raw, jnp.zeros((q - rows_in, LANES), F32)], axis=0)
    dt = _softplus(dt_raw + dtb_ref[...])
    if n_valid < q:
        dt = jnp.where(row < n_valid, dt, 0.0)
    adt = dt * (-jnp.exp(alog_ref[...]))
    acum = adt
    k = 1
    while k < q:
        acum = acum + jnp.where(row >= k, pltpu.roll(acum, k, axis=0), 0.0)
        k *= 2
    act_t_ref[...] = acum.T
    eac = jnp.exp(acum)
    dec = jnp.exp(acum[q - 1:q, :] - acum)
    for i, val in enumerate((dt, dt * dec, eac)):
        hi, mid = _pieces2(val)
        ex = jnp.dot(jnp.concatenate([hi, mid], axis=0), e_ref[...], preferred_element_type=F32)
        ex_ref[i] = ex[:q] + ex[q:]

    r_i = lax.broadcasted_iota(jnp.int32, (q, q), 0)
    c_i = lax.broadcasted_iota(jnp.int32, (q, q), 1)
    causal = r_i >= c_i
    blk_r = lax.broadcasted_iota(jnp.int32, (SSM_HPG * q, gw), 0) // q
    blk_c = lax.broadcasted_iota(jnp.int32, (SSM_HPG * q, gw), 1) // SSM_HEAD_DIM
    diag_blocks = blk_r == blk_c

    gn = SSM_GROUPS * SSM_STATE
    for g in range(SSM_GROUPS):
        gs = slice(g * gw, (g + 1) * gw)
        if lookahead:
            conv_cols(1 - cur, g * cchunk)
        bt = act_ref[cur, :, SSM_D_INNER + g * SSM_STATE:SSM_D_INNER + (g + 1) * SSM_STATE].T.astype(BF16)
        c_g = act_ref[cur, :, SSM_D_INNER + gn + g * SSM_STATE:SSM_D_INNER + gn + (g + 1) * SSM_STATE].astype(BF16)
        cb = jnp.dot(c_g, bt, preferred_element_type=F32)
        mats = []
        for r in range(SSM_HPG):
            h = g * SSM_HPG + r
            seg = acum[:, h:h + 1] - act_t_ref[h:h + 1, :]
            mats.append((cb * jnp.exp(jnp.where(causal, seg, -jnp.inf))).astype(BF16))
        mcat = jnp.concatenate(mats, axis=1)
        xs_g = act_ref[cur, :, gs]
        xdt = (xs_g * ex_ref[0, :, gs]).astype(BF16)
        bd = jnp.where(diag_blocks, jnp.concatenate([xdt] * SSM_HPG, axis=0), jnp.zeros((), BF16))
        y_diag = jnp.dot(mcat, bd, preferred_element_type=F32)
        ht = ht_ref[g]
        y_off = jnp.dot(c_g, ht.astype(BF16), preferred_element_type=F32)
        eac_g = ex_ref[2, :, gs]
        y = y_diag + y_off * eac_g + xs_g * dexp_ref[:, gs]
        y_ref[:, gs] = y[:rows_in]
        xdtd = (xs_g * ex_ref[1, :, gs]).astype(BF16)
        ht_ref[g] = ht * eac_g[q - 1:q, :] + jnp.dot(bt, xdtd, preferred_element_type=F32)

    if lookahead:
        @pl.when(c < last_c)
        def _():
            halo_ref[tail0:halo, :] = tail_of(xnx_ref)

    @pl.when(c == last_c)
    def _():
        for g in range(SSM_GROUPS):
            hout_ref[0, g * SSM_HPG:(g + 1) * SSM_HPG] = ht_ref[g].T.reshape(SSM_HPG, SSM_HEAD_DIM, SSM_STATE)


def _ssd(om, od, cw, cb, dtb, alog, dexp, emat, conv0, h0, nb, nc, rows_in, n_valid):
    m = om.shape[0]
    gw = SSM_HPG * SSM_HEAD_DIM
    kern = functools.partial(_ssd_kernel, rows_in=rows_in, n_valid=n_valid, lookahead=nc > 1)
    vec = lambda: pl.BlockSpec((1, LANES), lambda b, c: (0, 0))
    return pl.pallas_call(
        kern,
        grid=(nb, nc),
        in_specs=[
            pl.BlockSpec((rows_in, SSM_CONV_DIM), lambda b, c: (b * nc + c, 0)),
            pl.BlockSpec((rows_in, SSM_CONV_DIM), lambda b, c: (b * nc + jnp.minimum(c + 1, nc - 1), 0)),
            pl.BlockSpec((rows_in, LANES), lambda b, c: (b * nc + c, 0)),
            pl.BlockSpec((SSM_CONV, SSM_CONV_DIM), lambda b, c: (0, 0)),
            pl.BlockSpec((1, SSM_CONV_DIM), lambda b, c: (0, 0)),
            vec(), vec(),
            pl.BlockSpec((1, SSM_D_INNER), lambda b, c: (0, 0)),
            pl.BlockSpec((LANES, SSM_D_INNER), lambda b, c: (0, 0)),
            pl.BlockSpec((1, SSM_CONV - 1, SSM_CONV_DIM), lambda b, c: (b, 0, 0)),
            pl.BlockSpec((1, SSM_HEADS, SSM_HEAD_DIM, SSM_STATE), lambda b, c: (b, 0, 0, 0)),
        ],
        out_specs=[
            pl.BlockSpec((rows_in, SSM_D_INNER), lambda b, c: (b * nc + c, 0)),
            pl.BlockSpec((1, SSM_HEADS, SSM_HEAD_DIM, SSM_STATE), lambda b, c: (b, 0, 0, 0)),
            pl.BlockSpec((1, SSM_CONV - 1, SSM_CONV_DIM), lambda b, c: (b, 0, 0)),
        ],
        out_shape=[
            jax.ShapeDtypeStruct((m, SSM_D_INNER), F32),
            jax.ShapeDtypeStruct((nb, SSM_HEADS, SSM_HEAD_DIM, SSM_STATE), F32),
            jax.ShapeDtypeStruct((nb, SSM_CONV - 1, SSM_CONV_DIM), F32),
        ],
        scratch_shapes=[
            pltpu.VMEM((SUBLANES + SSM_CHUNK, SSM_CONV_DIM), F32),
            pltpu.VMEM((SUBLANES, SSM_CONV_DIM), F32),
            pltpu.VMEM((2, SSM_CHUNK, SSM_CONV_DIM), F32),
            pltpu.VMEM((SSM_GROUPS, SSM_STATE, gw), F32),
            pltpu.VMEM((LANES, SSM_CHUNK), F32),
            pltpu.VMEM((3, SSM_CHUNK, SSM_D_INNER), F32),
        ],
        compiler_params=_cparams(("arbitrary", "arbitrary")),
        name="ssd",
    )(om, om, od, cw, cb, dtb, alog, dexp, emat, conv0, h0)


def _outproj0_kernel(y_ref, z_ref, nw_ref, w_ref, np_ref, x_ref, o_ref):
    h = y_ref[...] * _silu(z_ref[...])
    gw = SSM_D_INNER // SSM_GROUPS
    hn = jnp.concatenate([_rms(h[:, g * gw:(g + 1) * gw]) for g in range(SSM_GROUPS)], axis=1)
    hn = (hn * nw_ref[...]).astype(BF16)
    o = jnp.dot(hn, w_ref[...], preferred_element_type=F32)
    o_ref[...] = x_ref[...] + _rms(o) * np_ref[...]


def _outproj0(y, om, nw, w, npost, x, tm):
    m = y.shape[0]
    zblk = SSM_CONV_DIM // SSM_D_INNER
    return pl.pallas_call(
        _outproj0_kernel,
        grid=(m // tm,),
        in_specs=[
            pl.BlockSpec((tm, SSM_D_INNER), lambda i: (i, 0)),
            pl.BlockSpec((tm, SSM_D_INNER), lambda i: (i, zblk)),
            pl.BlockSpec((1, SSM_D_INNER), lambda i: (0, 0)),
            pl.BlockSpec((SSM_D_INNER, D_MODEL), lambda i: (0, 0)),
            pl.BlockSpec((1, D_MODEL), lambda i: (0, 0)),
            pl.BlockSpec((tm, D_MODEL), lambda i: (i, 0)),
        ],
        out_specs=pl.BlockSpec((tm, D_MODEL), lambda i: (i, 0)),
        out_shape=jax.ShapeDtypeStruct((m, D_MODEL), F32),
        compiler_params=_cparams(("parallel",)),
        name="outproj0",
    )(y, om, nw, w, npost, x)


def _inproj1p_kernel(x_ref, g_ref, wt_ref, wft_ref, bf_ref, qt_ref, kt_ref, vt_ref, vtb_ref, zt_ref, lft_ref):
    un = (_rms(x_ref[...]) * g_ref[...]).astype(BF16)
    w = ATTN_WIDTH
    proj = lambda i: lax.dot_general(wt_ref[i * w:(i + 1) * w, :], un, _NT, preferred_element_type=F32)
    qt_ref[0] = (proj(0) * (ATTN_HEAD_DIM ** -0.5 * LOG2E)).astype(BF16)
    kt_ref[0] = proj(1)
    vt = proj(2)
    vt_ref[0] = vt
    vtb_ref[0] = vt.astype(BF16)
    zt_ref[0] = proj(3)
    ft = lax.dot_general(wft_ref[...], un, _NT, preferred_element_type=F32)
    lft_ref[0] = _log_sigmoid(ft + bf_ref[...])


def _inproj1p(x, g, wt, wft, bfc, nb, seq, tm):
    m, d = x.shape
    nq = seq // tm
    w = ATTN_WIDTH
    full = lambda r, c: pl.BlockSpec((r, c), lambda b, i: (0, 0))
    feat = lambda r: pl.BlockSpec((1, r, tm), lambda b, i: (b, 0, i))
    fshape = lambda dt: jax.ShapeDtypeStruct((nb, w, seq), dt)
    return pl.pallas_call(
        _inproj1p_kernel,
        grid=(nb, nq),
        in_specs=[pl.BlockSpec((tm, d), lambda b, i: (b * nq + i, 0)), full(1, d), full(4 * w, d),
                  full(ATTN_HEADS, d), full(ATTN_HEADS, 1)],
        out_specs=[feat(w), feat(w), feat(w), feat(w), feat(w), feat(ATTN_HEADS)],
        out_shape=[fshape(BF16), fshape(F32), fshape(F32), fshape(BF16), fshape(F32),
                   jax.ShapeDtypeStruct((nb, ATTN_HEADS, seq), F32)],
        compiler_params=_cparams(("parallel", "parallel")),
        name="inproj1p",
    )(x, g, wt, wft, bfc)


def _inproj1s_kernel(x_ref, g_ref, wt_ref, wf_ref, bf_ref, o_ref, lf_ref):
    un = (_rms(x_ref[...]) * g_ref[...]).astype(BF16)
    o_ref[...] = lax.dot_general(un, wt_ref[...], _NT, preferred_element_type=F32)
    lf_ref[...] = _log_sigmoid(jnp.dot(un, wf_ref[...], preferred_element_type=F32) + bf_ref[...])


def _inproj1s(x, g, wt, wf, bfr):
    m, d = x.shape
    n = wt.shape[0]
    return pl.pallas_call(
        _inproj1s_kernel,
        grid=(1,),
        in_specs=[pl.BlockSpec((m, d), lambda i: (0, 0)), pl.BlockSpec((1, d), lambda i: (0, 0)),
                  pl.BlockSpec((n, d), lambda i: (0, 0)), pl.BlockSpec((d, LANES), lambda i: (0, 0)),
                  pl.BlockSpec((1, LANES), lambda i: (0, 0))],
        out_specs=[pl.BlockSpec((m, n), lambda i: (0, 0)), pl.BlockSpec((m, LANES), lambda i: (0, 0))],
        out_shape=[jax.ShapeDtypeStruct((m, n), F32), jax.ShapeDtypeStruct((m, LANES), F32)],
        compiler_params=_cparams(("arbitrary",)),
        name="inproj1s",
    )(x, g, wt, wf, bfr)


def _cumsum_kernel(x_ref, o_ref):
    nblk = x_ref.shape[2] // LANES
    upper = _tri(LANES, lambda r, c: r <= c)

    def body(j, carry):
        off = pl.multiple_of(j * LANES, LANES)
        loc = _dot3(x_ref[0, :, pl.ds(off, LANES)], upper)
        o_ref[0, :, pl.ds(off, LANES)] = loc + carry
        return carry + loc[:, LANES - 1:LANES]

    lax.fori_loop(0, nblk, body, jnp.zeros((x_ref.shape[1], 1), F32))


def _cumsum(x):
    nb, h, seq = x.shape
    return pl.pallas_call(
        _cumsum_kernel,
        grid=(nb,),
        in_specs=[pl.BlockSpec((1, h, seq), lambda b: (b, 0, 0))],
        out_specs=pl.BlockSpec((1, h, seq), lambda b: (b, 0, 0)),
        out_shape=jax.ShapeDtypeStruct((nb, h, seq), F32),
        compiler_params=_cparams(("parallel",)),
        name="lf_cumsum",
    )(x)


def _bias_rows(c2, first, n, width):
    hi, mid, lo = (p.astype(F32) for p in _split3(c2))
    row = lax.broadcasted_iota(jnp.int32, (n, width), 0)
    ones_first = 3 - first
    out = jnp.where((row >= ones_first) & (row < ones_first + 3), 1.0, 0.0)
    for i, piece in enumerate((hi, mid, lo)):
        out = jnp.where(row == first + i, piece, out)
    return out


def _kaug_kernel(kt_ref, c_ref, o_ref, *, chunk):
    h = pl.program_id(1)

    def body(j, carry):
        off = pl.multiple_of(j * chunk, chunk)
        c2 = c_ref[0, pl.ds(h, 1), pl.ds(off, chunk)] * LOG2E
        full = jnp.concatenate([kt_ref[0, :, pl.ds(off, chunk)],
                                _bias_rows(-c2, 3, ATTN_HEAD_DIM, chunk)], axis=0)
        o_ref[0, 0, pl.ds(off, chunk), :] = full.T.astype(BF16)
        return carry

    lax.fori_loop(0, kt_ref.shape[2] // chunk, body, 0)


def _kaug(kt, c, chunk):
    nb, w, seq = kt.shape
    return pl.pallas_call(
        functools.partial(_kaug_kernel, chunk=chunk),
        grid=(nb, ATTN_HEADS),
        in_specs=[pl.BlockSpec((1, ATTN_HEAD_DIM, seq), lambda b, h: (b, h, 0)),
                  pl.BlockSpec((1, ATTN_HEADS, seq), lambda b, h: (b, 0, 0))],
        out_specs=pl.BlockSpec((1, 1, seq, LANES), lambda b, h: (b, h, 0, 0)),
        out_shape=jax.ShapeDtypeStruct((nb, ATTN_HEADS, seq, LANES), BF16),
        compiler_params=_cparams(("parallel", "parallel")),
        name="fox_kaug",
    )(kt, c)


def _flash_body(qi, kj, qt_ref, ka_ref, vt_ref, cq_ref, o_ref, qa_ref, m_ref, acc_ref, s_ref, *, tq, tk, between):
    hd = ATTN_HEAD_DIM

    @pl.when(kj == 0)
    def _():
        m_ref[...] = jnp.full(m_ref.shape, NEG_BIG, F32)
        acc_ref[...] = jnp.zeros(acc_ref.shape, F32)

        def build(h, carry):
            off = pl.multiple_of(h * hd, hd)
            c2 = cq_ref[0, pl.ds(h, 1), :] * LOG2E
            qa_ref[h, 0:hd, :] = qt_ref[0, pl.ds(off, hd), :]
            qa_ref[h, hd:2 * hd, :] = _bias_rows(c2, 0, hd, tq).astype(BF16)
            return carry

        lax.fori_loop(0, ATTN_HEADS, build, 0)

    def step(masked):
        if masked:
            key = kj * tk + lax.broadcasted_iota(jnp.int32, (tk, tq), 0)
            qry = qi * tq + lax.broadcasted_iota(jnp.int32, (tk, tq), 1)
            keep = key <= qry
        ones = jnp.ones((AUG, tk), BF16)

        def scores(h, slot):
            st = jnp.dot(ka_ref[0, h], qa_ref[h], preferred_element_type=F32)
            if masked:
                st = jnp.where(keep, st, NEG_BIG)
            s_ref[slot] = st
            return jnp.max(st, axis=0, keepdims=True)

        def softmax_pv(h, slot, mx):
            off = pl.multiple_of(h * hd, hd)
            m_prev = m_ref[h]
            m_new = jnp.maximum(m_prev, mx)
            pt = jnp.exp2(s_ref[slot] - m_new[0:1]).astype(BF16)
            va = jnp.concatenate([vt_ref[0, pl.ds(off, hd), :], ones], axis=0)
            pv = jnp.dot(va, pt, preferred_element_type=F32)
            acc_ref[h] = acc_ref[h] * jnp.exp2(m_prev - m_new)[0:1] + pv
            m_ref[h] = m_new

        mx = scores(0, 0)
        for h in range(ATTN_HEADS):
            nxt = scores(h + 1, (h + 1) % 2) if h + 1 < ATTN_HEADS else None
            between(h)
            softmax_pv(h, h % 2, mx)
            mx = nxt

    @pl.when(kj < qi)
    def _():
        step(False)

    @pl.when(kj == qi)
    def _():
        step(True)

        def fin(h, carry):
            off = pl.multiple_of(h * hd, hd)
            a = acc_ref[h]
            o_ref[0, pl.ds(off, hd), :] = a[0:hd] / a[hd:hd + 1]
            return carry

        lax.fori_loop(0, ATTN_HEADS, fin, 0)


class _Decode:
    def __init__(self, q_ref, kn_ref, vn_ref, lfn_ref, k_refs, v_refs, lf_refs,
                 o_ref, qg_ref, m_ref, l_ref, acc_ref, carry_ref, lq):
        self.__dict__.update(locals())
        self.hd = ATTN_HEAD_DIM
        self.ngrp = DECODE_HEAD_GROUPS
        self.hpg = ATTN_HEADS // self.ngrp
        self.grow = self.hpg * lq
        self.gcol = self.hpg * self.hd
        self.nrow = ATTN_HEADS * lq

    def rsl(self, g):
        return slice(g * self.grow, (g + 1) * self.grow)

    def csl(self, g):
        return slice(g * self.gcol, (g + 1) * self.gcol)

    def init(self):
        lq, hd = self.lq, self.hd
        rh = lax.broadcasted_iota(jnp.int32, (self.grow, self.gcol), 0) // lq
        ch = lax.broadcasted_iota(jnp.int32, (self.grow, self.gcol), 1) // hd
        for g in range(self.ngrp):
            qt = jnp.concatenate([self.q_ref[0, :, self.csl(g)] * (hd ** -0.5)] * self.hpg, axis=0)
            self.qg_ref[g] = jnp.where(rh == ch, qt, 0.0).astype(BF16)
        self.m_ref[...] = jnp.full(self.m_ref.shape, NEG_BIG, F32)
        self.l_ref[...] = jnp.zeros(self.l_ref.shape, F32)
        self.acc_ref[...] = jnp.zeros(self.acc_ref.shape, F32)
        self.carry_ref[...] = jnp.zeros(self.carry_ref.shape, F32)

    def expand_heads(self, r):
        return jnp.concatenate(
            [jnp.broadcast_to(r[h:h + 1, :], (self.lq, r.shape[1])) for h in range(ATTN_HEADS)], axis=0)

    def merge(self, s_list, pv_fn):
        m_prev = self.m_ref[...]
        m_new = m_prev
        for s in s_list:
            m_new = jnp.maximum(m_new, jnp.max(s, axis=1, keepdims=True))
        alpha = jnp.exp(m_prev - m_new)
        lsum = jnp.zeros_like(m_prev)
        pv = [None] * self.ngrp
        for i, s in enumerate(s_list):
            p = jnp.exp(s - m_new)
            lsum = lsum + jnp.sum(p, axis=1, keepdims=True)
            pb = p.astype(BF16)
            for g in range(self.ngrp):
                d = pv_fn(i, g, pb[self.rsl(g)])
                pv[g] = d if pv[g] is None else pv[g] + d
        self.l_ref[...] = alpha * self.l_ref[...] + lsum
        for g in range(self.ngrp):
            self.acc_ref[g] = alpha[self.rsl(g)] * self.acc_ref[g] + pv[g]
        self.m_ref[...] = m_new

    def pages(self, plist, active):
        later = _tri(PAGE_SIZE, lambda r, c: r > c)
        carry = self.carry_ref[...]
        s_list, v_list = [], []
        for p in plist:
            lf = jnp.where(active, self.lf_refs[p][0], 0.0)
            r = _dot3(lf, later) + carry
            carry = carry + jnp.sum(lf, axis=1, keepdims=True)
            kt = self.k_refs[p][0].astype(BF16)
            s = jnp.concatenate([jnp.dot(self.qg_ref[g], kt[self.csl(g)], preferred_element_type=F32)
                                 for g in range(self.ngrp)], axis=0)
            s_list.append(jnp.where(active, s + self.expand_heads(r), NEG_BIG))
            v_list.append(self.v_refs[p][0].astype(BF16))
        self.carry_ref[...] = carry
        self.merge(s_list, lambda i, g, pr: lax.dot_general(
            pr, v_list[i][self.csl(g)], _NT, preferred_element_type=F32))

    def finish(self):
        lq, hd = self.lq, self.hd
        pad = jnp.zeros((LANES - lq, ATTN_WIDTH), F32)
        kn = jnp.concatenate([self.kn_ref[0], pad], axis=0).astype(BF16)
        vn = jnp.concatenate([self.vn_ref[0], pad], axis=0).astype(BF16)
        s = jnp.concatenate([lax.dot_general(self.qg_ref[g], kn[:, self.csl(g)], _NT, preferred_element_type=F32)
                             for g in range(self.ngrp)], axis=0)
        cn = _dot3(self.lfn_ref[0], _tri(LANES, lambda r, c: r <= c))
        s = s - self.expand_heads(cn)
        qpos = lax.broadcasted_iota(jnp.int32, (self.nrow, LANES), 0) % lq
        kpos = lax.broadcasted_iota(jnp.int32, (self.nrow, LANES), 1)
        s = jnp.where(kpos <= qpos, s, NEG_BIG)
        self.merge([s], lambda i, g, pr: jnp.dot(pr, vn[:, self.csl(g)], preferred_element_type=F32))
        for g in range(self.ngrp):
            out = self.acc_ref[g] / self.l_ref[self.rsl(g), :]
            for hl in range(self.hpg):
                h = g * self.hpg + hl
                self.o_ref[0, :, h * hd:(h + 1) * hd] = out[hl * lq:(hl + 1) * lq, hl * hd:(hl + 1) * hd]


def _attn_kernel(bt_ref, qit_ref, kjt_ref, pt_ref, qt_ref, ka_ref, vt_ref, cq_ref,
                 qs_ref, kn_ref, vn_ref, lfn_ref, *refs, tq, tk, pps, lq, n_dec, spe):
    del bt_ref, pt_ref
    k_refs = refs[0:pps]
    v_refs = refs[pps:2 * pps]
    lf_refs = refs[2 * pps:3 * pps]
    (o_ref, os_ref, qa_ref, m_ref, acc_ref, s_ref, qg_ref, md_ref, ld_ref, accd_ref, carry_ref) = refs[3 * pps:]
    t = pl.program_id(0)
    active = t < n_dec
    js = t % spe
    dec = _Decode(qs_ref, kn_ref, vn_ref, lfn_ref, k_refs, v_refs, lf_refs,
                  os_ref, qg_ref, md_ref, ld_ref, accd_ref, carry_ref, lq)

    @pl.when(active & (js == 0))
    def _():
        dec.init()

    nbatch = pl.cdiv(pps, DECODE_PAGES_PER_MERGE)
    stride = ATTN_HEADS // nbatch

    def pages_between(h):
        if h % stride == 0 and h // stride < nbatch:
            b0 = (h // stride) * DECODE_PAGES_PER_MERGE
            dec.pages(range(b0, min(b0 + DECODE_PAGES_PER_MERGE, pps)), active)

    _flash_body(qit_ref[t], kjt_ref[t], qt_ref, ka_ref, vt_ref, cq_ref, o_ref, qa_ref, m_ref, acc_ref, s_ref,
                tq=tq, tk=tk, between=pages_between)

    @pl.when(active & (js == spe - 1))
    def _():
        dec.finish()


def _attention(qt, ka, vtb, c, q_s, kn, vn, lfn_t, ck, cv, clf, page_table, tq, pps):
    nb, w, seq = qt.shape
    nbs, lq, _ = q_s.shape
    npages = page_table.shape[1]
    nq = seq // tq
    tri = [(b, i, j) for b in range(nb) for i in range(nq) for j in range(i + 1)]
    nsteps = len(tri)
    spe = npages // pps
    n_dec = nbs * spe
    assert n_dec <= nsteps and npages % pps == 0, (n_dec, nsteps)
    bt, qit, kjt = (jnp.asarray([x[k] for x in tri], jnp.int32) for k in range(3))
    nrow = ATTN_HEADS * lq
    ng = DECODE_HEAD_GROUPS
    kern = functools.partial(_attn_kernel, tq=tq, tk=tq, pps=pps, lq=lq, n_dec=n_dec, spe=spe)

    def dec_elem(t):
        return jnp.minimum(t // spe, nbs - 1)

    def page_spec(shape, p):
        def imap(t, bt, qit, kjt, pt):
            td = jnp.minimum(t, n_dec - 1)
            return (pt[td // spe, npages - 1 - ((td % spe) * pps + p)], 0, 0)
        return pl.BlockSpec(shape, imap)

    tokspec = lambda: pl.BlockSpec((1, lq, w), lambda t, bt, qit, kjt, pt: (dec_elem(t), 0, 0))
    in_specs = [
        pl.BlockSpec((1, w, tq), lambda t, bt, qit, kjt, pt: (bt[t], 0, qit[t])),
        pl.BlockSpec((1, ATTN_HEADS, tq, LANES), lambda t, bt, qit, kjt, pt: (bt[t], 0, kjt[t], 0)),
        pl.BlockSpec((1, w, tq), lambda t, bt, qit, kjt, pt: (bt[t], 0, kjt[t])),
        pl.BlockSpec((1, ATTN_HEADS, tq), lambda t, bt, qit, kjt, pt: (bt[t], 0, qit[t])),
        tokspec(), tokspec(), tokspec(),
        pl.BlockSpec((1, ATTN_HEADS, LANES), lambda t, bt, qit, kjt, pt: (dec_elem(t), 0, 0)),
    ]
    in_specs += [page_spec((1, w, PAGE_SIZE), p) for p in range(pps)]
    in_specs += [page_spec((1, w, PAGE_SIZE), p) for p in range(pps)]
    in_specs += [page_spec((1, ATTN_HEADS, PAGE_SIZE), p) for p in range(pps)]
    return pl.pallas_call(
        kern,
        grid_spec=pltpu.PrefetchScalarGridSpec(
            num_scalar_prefetch=4,
            grid=(nsteps,),
            in_specs=in_specs,
            out_specs=[
                pl.BlockSpec((1, w, tq), lambda t, bt, qit, kjt, pt: (bt[t], 0, qit[t])),
                pl.BlockSpec((1, lq, w), lambda t, bt, qit, kjt, pt: (dec_elem(t), 0, 0)),
            ],
            scratch_shapes=[
                pltpu.VMEM((ATTN_HEADS, 2 * ATTN_HEAD_DIM, tq), BF16),
                pltpu.VMEM((ATTN_HEADS, SUBLANES, tq), F32),
                pltpu.VMEM((ATTN_HEADS, ATTN_HEAD_DIM + AUG, tq), F32),
                pltpu.VMEM((2, tq, tq), F32),
                pltpu.VMEM((ng, nrow // ng, w // ng), BF16),
                pltpu.VMEM((nrow, 1), F32),
                pltpu.VMEM((nrow, 1), F32),
                pltpu.VMEM((ng, nrow // ng, w // ng), F32),
                pltpu.VMEM((ATTN_HEADS, 1), F32),
            ],
        ),
        out_shape=[jax.ShapeDtypeStruct((nb, w, seq), F32), jax.ShapeDtypeStruct((nbs, lq, w), F32)],
        compiler_params=pltpu.CompilerParams(dimension_semantics=("arbitrary",), vmem_limit_bytes=ATTN_VMEM_LIMIT),
        name="fox_attention",
    )(bt, qit, kjt, page_table, qt, ka, vtb, c, q_s, kn, vn, lfn_t, *([ck] * pps), *([cv] * pps), *([clf] * pps))


def _outproj1_kernel(o_ref, z_ref, w_ref, np_ref, x_ref, out_ref):
    g = (o_ref[...] * _silu(z_ref[...])).astype(BF16)
    y = jnp.dot(g, w_ref[...], preferred_element_type=F32)
    out_ref[...] = x_ref[...] + _rms(y) * np_ref[...]


def _outproj1(o, z, zblk, w, npost, x, tm):
    m = o.shape[0]
    return pl.pallas_call(
        _outproj1_kernel,
        grid=(m // tm,),
        in_specs=[
            pl.BlockSpec((tm, ATTN_WIDTH), lambda i: (i, 0)),
            pl.BlockSpec((tm, ATTN_WIDTH), lambda i: (i, zblk)),
            pl.BlockSpec((ATTN_WIDTH, D_MODEL), lambda i: (0, 0)),
            pl.BlockSpec((1, D_MODEL), lambda i: (0, 0)),
            pl.BlockSpec((tm, D_MODEL), lambda i: (i, 0)),
        ],
        out_specs=pl.BlockSpec((tm, D_MODEL), lambda i: (i, 0)),
        out_shape=jax.ShapeDtypeStruct((m, D_MODEL), F32),
        compiler_params=_cparams(("parallel",)),
        name="outproj1",
    )(o, z, w, npost, x)


def _outproj1p_kernel(ot_ref, zt_ref, w_ref, np_ref, x_ref, out_ref):
    gt = (ot_ref[0] * _silu(zt_ref[0])).astype(BF16)
    y = lax.dot_general(gt, w_ref[...], _TN, preferred_element_type=F32)
    out_ref[...] = x_ref[...] + _rms(y) * np_ref[...]


def _outproj1p(ot, zt, w, npost, x, tm):
    nb, wd, seq = ot.shape
    nq = seq // tm
    feat = lambda: pl.BlockSpec((1, wd, tm), lambda b, i: (b, 0, i))
    return pl.pallas_call(
        _outproj1p_kernel,
        grid=(nb, nq),
        in_specs=[feat(), feat(),
                  pl.BlockSpec((wd, D_MODEL), lambda b, i: (0, 0)),
                  pl.BlockSpec((1, D_MODEL), lambda b, i: (0, 0)),
                  pl.BlockSpec((tm, D_MODEL), lambda b, i: (b * nq + i, 0))],
        out_specs=pl.BlockSpec((tm, D_MODEL), lambda b, i: (b * nq + i, 0)),
        out_shape=jax.ShapeDtypeStruct((nb * seq, D_MODEL), F32),
        compiler_params=_cparams(("parallel", "parallel")),
        name="outproj1p",
    )(ot, zt, w, npost, x)


def _pad_lanes(v):
    return jnp.pad(v.astype(F32), (0, LANES - v.shape[0])).reshape(1, LANES)


def kernel(x_prompt, x_sample, state_ssm, state_conv, cache_k, cache_v, cache_logf, page_table, norm_pre, norm_post, ssm_w_in, ssm_conv_w, ssm_conv_b, ssm_dt_bias, ssm_a_log, ssm_d, ssm_norm_w, ssm_w_out, attn_w_in, attn_b_f, attn_w_out):
    nbp, seq, d = x_prompt.shape
    nbs, lq, _ = x_sample.shape
    xp = x_prompt.reshape(nbp * seq, d)
    xs = x_sample.reshape(nbs * lq, d)
    tm_p = min(1024, seq)

    w_in = ssm_w_in[0]
    z_end = SSM_D_INNER
    xbc_end = SSM_D_INNER + SSM_CONV_DIM
    wm = jnp.concatenate([w_in[:, z_end:xbc_end], w_in[:, :z_end]], axis=1).astype(BF16)
    wd = jnp.pad(w_in[:, xbc_end:], ((0, 0), (0, LANES - SSM_HEADS))).astype(BF16)
    g0 = norm_pre[0:1]
    head_of_lane = jnp.arange(SSM_D_INNER, dtype=jnp.int32) // SSM_HEAD_DIM
    expand = (jnp.arange(LANES, dtype=jnp.int32)[:, None] == head_of_lane[None, :]).astype(BF16)
    d_lanes = jnp.repeat(ssm_d[0].astype(F32), SSM_HEAD_DIM).reshape(1, SSM_D_INNER)
    ssd_vecs = (ssm_conv_w[0], ssm_conv_b[0:1], _pad_lanes(ssm_dt_bias[0]), _pad_lanes(ssm_a_log[0]),
                d_lanes, expand)
    w_out0 = ssm_w_out[0].astype(BF16)
    nw0 = ssm_norm_w[0:1]
    np0 = norm_post[0:1]

    om_p, od_p = _inproj0(xp, g0, wm, wd, tm_p)
    om_s, od_s = _inproj0(xs, g0, wm, wd, nbs * lq)

    conv0_p = jnp.zeros((nbp, SSM_CONV - 1, SSM_CONV_DIM), F32)
    h0_p = jnp.zeros((nbp, SSM_HEADS, SSM_HEAD_DIM, SSM_STATE), F32)
    y_p, ssm_p, conv_p = _ssd(om_p, od_p, *ssd_vecs, conv0_p, h0_p, nbp, seq // SSM_CHUNK, SSM_CHUNK, SSM_CHUNK)
    y_s, ssm_s, conv_s = _ssd(om_s, od_s, *ssd_vecs, state_conv[0], state_ssm[0], nbs, 1, lq, lq)

    hp1 = _outproj0(y_p, om_p, nw0, w_out0, np0, xp, min(512, seq))
    hs1 = _outproj0(y_s, om_s, nw0, w_out0, np0, xs, nbs * lq)

    wa = attn_w_in[0]
    w = ATTN_WIDTH
    wa_t = wa.T.astype(BF16)
    wf = jnp.pad(wa[:, 4 * w:], ((0, 0), (0, LANES - ATTN_HEADS))).astype(BF16)
    g1 = norm_pre[1:2]
    np1 = norm_post[1:2]
    w_out1 = attn_w_out[0].astype(BF16)

    qt, kt, vt, vtb, zt, lft = _inproj1p(
        hp1, g1, wa_t[:4 * w], wa_t[4 * w:], attn_b_f[0].astype(F32).reshape(ATTN_HEADS, 1),
        nbp, seq, min(512, seq))
    c_p = _cumsum(lft)
    ka = _kaug(kt, c_p, min(2048, seq))
    tq = min(512, seq)

    proj_s, lf_s = _inproj1s(hs1, g1, wa_t[:4 * w], wf, _pad_lanes(attn_b_f[0]))
    q_s = proj_s[:, :w].reshape(nbs, lq, w)
    k_s = proj_s[:, w:2 * w].reshape(nbs, lq, w)
    v_s = proj_s[:, 2 * w:3 * w].reshape(nbs, lq, w)
    lf_s = lf_s[:, :ATTN_HEADS].reshape(nbs, lq, ATTN_HEADS)
    lfn_t = jnp.pad(lf_s.transpose(0, 2, 1), ((0, 0), (0, 0), (0, LANES - lq)))
    npool = cache_k.shape[1]
    ck = cache_k[0].transpose(0, 2, 3, 1).reshape(npool, w, PAGE_SIZE)
    cv = cache_v[0].transpose(0, 2, 3, 1).reshape(npool, w, PAGE_SIZE)
    clf = cache_logf[0].transpose(0, 2, 1)
    ot, o_s = _attention(qt, ka, vtb, c_p, q_s, k_s, v_s, lfn_t, ck, cv, clf, page_table, tq,
                         min(DECODE_PAGES_PER_STEP, page_table.shape[1]))
    y_prompt = _outproj1p(ot, zt, w_out1, np1, hp1, min(512, seq))
    y_sample = _outproj1(o_s.reshape(nbs * lq, w), proj_s, 3, w_out1, np1, hs1, nbs * lq)

    hd = ATTN_HEAD_DIM
    return (
        y_prompt.reshape(nbp, seq, d),
        y_sample.reshape(nbs, lq, d),
        kt.reshape(nbp, ATTN_HEADS, hd, seq).transpose(0, 3, 1, 2)[None],
        vt.reshape(nbp, ATTN_HEADS, hd, seq).transpose(0, 3, 1, 2)[None],
        lft.transpose(0, 2, 1)[None],
        k_s.reshape(1, nbs, lq, ATTN_HEADS, hd),
        v_s.reshape(1, nbs, lq, ATTN_HEADS, hd),
        lf_s[None],
        ssm_p[None],
        conv_p[None],
        ssm_s[None],
        conv_s[None],
    )
```
